```python
import jax, jax.numpy as jnp
from jax import lax
import numpy as np

D_MODEL = 1024
BATCH = 2
SEQ = 16384
DEPTH = 2

N_A = DEPTH // 2
N_B = DEPTH - N_A
N_DENSE = (DEPTH + 1) // 2
N_MOE = DEPTH // 2

POOL_WINDOWS = (2, 4, 8, 16)
N_POOL_GROUPS = len(POOL_WINDOWS)
POOL_GROUP = D_MODEL // N_POOL_GROUPS

HEAD_DIM = 64
N_HEADS = D_MODEL // HEAD_DIM
Q_BLOCK = 128
NEG_INF = -1e30

D_FF_DENSE = ((8 * D_MODEL // 3 + 255) // 256) * 256
N_EXPERTS = 8
TOP_K = 2
D_FF_EXPERT = 7 * D_MODEL // 2

EPS = 1e-5

kernel_name = "yoco_pool_fox_moe_hybrid"


def rmsnorm(x, g):
    xf = x.astype(jnp.float32)
    y = xf * lax.rsqrt(jnp.mean(xf * xf, axis=-1, keepdims=True) + EPS)
    return (y * g.astype(jnp.float32)).astype(x.dtype)


def causal_multiscale_pool(h, w_grp, scale):
    B, S, D = h.shape
    hf = h.astype(jnp.float32).reshape(B, S, N_POOL_GROUPS, POOL_GROUP)
    csum = jnp.pad(jnp.cumsum(hf, axis=1), ((0, 0), (1, 0), (0, 0), (0, 0)))
    pos = jnp.arange(S)
    pooled = []
    for g, w in enumerate(POOL_WINDOWS):
        cg = csum[:, :, g]
        lo = jnp.maximum(pos + 1 - w, 0)
        win_sum = cg[:, 1:] - cg[:, lo]
        cnt = (pos + 1 - lo).astype(jnp.float32)
        pooled.append(win_sum / cnt[None, :, None])
    pooled = jnp.stack(pooled, axis=2) - hf
    mixed = jnp.einsum('bsgc,gcd->bsgd', pooled, w_grp.astype(jnp.float32))
    return (mixed.reshape(B, S, D) * scale.astype(jnp.float32)).astype(h.dtype)


def forgetting_attention(q, k, v, fcum):
    B, H, S, Dh = q.shape
    n_blocks = S // Q_BLOCK
    qf = q.astype(jnp.float32) * (Dh ** -0.5)
    kf = k.astype(jnp.float32)
    vf = v.astype(jnp.float32)
    offs = jnp.arange(Q_BLOCK)

    def one_query_block(i):
        q0 = i * Q_BLOCK
        qb = lax.dynamic_slice_in_dim(qf, q0, Q_BLOCK, axis=2)
        fq = lax.dynamic_slice_in_dim(fcum, q0, Q_BLOCK, axis=2)
        qpos = q0 + offs

        def body(j, carry):
            m, l, acc = carry
            k0 = j * Q_BLOCK
            kb = lax.dynamic_slice_in_dim(kf, k0, Q_BLOCK, axis=2)
            vb = lax.dynamic_slice_in_dim(vf, k0, Q_BLOCK, axis=2)
            fk = lax.dynamic_slice_in_dim(fcum, k0, Q_BLOCK, axis=2)
            kpos = k0 + offs
            s = (jnp.einsum('bhqd,bhkd->bhqk', qb, kb)
                 + fq[..., :, None] - fk[..., None, :])
            s = jnp.where(qpos[:, None] >= kpos[None, :], s, NEG_INF)
            m_new = jnp.maximum(m, jnp.max(s, axis=-1))
            p = jnp.exp(s - m_new[..., None])
            corr = jnp.exp(m - m_new)
            l = l * corr + jnp.sum(p, axis=-1)
            acc = acc * corr[..., None] + jnp.einsum('bhqk,bhkd->bhqd', p, vb)
            return m_new, l, acc

        init = (jnp.full((B, H, Q_BLOCK), NEG_INF, jnp.float32),
                jnp.zeros((B, H, Q_BLOCK), jnp.float32),
                jnp.zeros((B, H, Q_BLOCK, Dh), jnp.float32))
        _, l, acc = lax.fori_loop(0, i + 1, body, init)
        return acc / l[..., None]

    out = lax.map(one_query_block, jnp.arange(n_blocks))
    out = jnp.moveaxis(out, 0, 2).reshape(B, H, S, Dh)
    return out.astype(q.dtype)


def swiglu(h, w_gate, w_up, w_down):
    return (jax.nn.silu(h @ w_gate) * (h @ w_up)) @ w_down


def moe_swiglu(h, w_router, e_gate, e_up, e_down):
    logits = (h @ w_router).astype(jnp.float32)
    top_v, top_i = lax.top_k(logits, TOP_K)
    wts = jax.nn.softmax(top_v, axis=-1)
    gates = jnp.sum(jax.nn.one_hot(top_i, N_EXPERTS, dtype=jnp.float32)
                    * wts[..., None], axis=-2)
    out = jnp.zeros(h.shape, jnp.float32)
    for e in range(N_EXPERTS):
        y = swiglu(h, e_gate[e], e_up[e], e_down[e]).astype(jnp.float32)
        out = out + gates[..., e:e + 1] * y
    return out.astype(h.dtype)


def setup_inputs(seed: int = 0) -> dict:
    key = jax.random.key(seed)
    ks = jax.random.split(key, 20)
    f32 = jnp.float32
    D, H = D_MODEL, N_HEADS

    def nrm(k, shape, fan_in):
        return jax.random.normal(k, shape, f32) * (fan_in ** -0.5)

    def gain(k, shape):
        return 1.0 + 0.02 * jax.random.normal(k, shape, f32)

    b_f = jnp.linspace(1.0, 5.0, H, dtype=f32) + 0.1 * jax.random.normal(ks[7], (H,), f32)
    return {
        "x": jax.random.normal(ks[0], (BATCH, SEQ, D), f32),
        "norm_mix": gain(ks[1], (DEPTH, D)),
        "norm_ffn": gain(ks[2], (DEPTH, D)),
        "pool_w": nrm(ks[3], (N_A, N_POOL_GROUPS, POOL_GROUP, POOL_GROUP), POOL_GROUP),
        "pool_scale": gain(ks[4], (N_A, D)),
        "norm_kv": gain(ks[5], (D,)),
        "w_kv": nrm(ks[6], (D, 2 * D + H), D),
        "b_f": b_f,
        "w_q": nrm(ks[8], (N_B, D, D), D),
        "w_o": nrm(ks[9], (N_B, D, D), D),
        "ffn_gate": nrm(ks[10], (N_DENSE, D, D_FF_DENSE), D),
        "ffn_up": nrm(ks[11], (N_DENSE, D, D_FF_DENSE), D),
        "ffn_down": nrm(ks[12], (N_DENSE, D_FF_DENSE, D), D_FF_DENSE),
        "w_router": nrm(ks[13], (N_MOE, D, N_EXPERTS), D),
        "exp_gate": nrm(ks[14], (N_MOE, N_EXPERTS, D, D_FF_EXPERT), D),
        "exp_up": nrm(ks[15], (N_MOE, N_EXPERTS, D, D_FF_EXPERT), D),
        "exp_down": nrm(ks[16], (N_MOE, N_EXPERTS, D_FF_EXPERT, D), D_FF_EXPERT),
        "norm_final": gain(ks[17], (D,)),
    }


def reference(x, norm_mix, norm_ffn, pool_w, pool_scale, norm_kv, w_kv, b_f,
              w_q, w_o, ffn_gate, ffn_up, ffn_down, w_router, exp_gate, exp_up,
              exp_down, norm_final):
    B, S, D = x.shape
    H, Dh = N_HEADS, HEAD_DIM
    k_sh = v_sh = fcum_sh = None
    for l in range(DEPTH):
        h = rmsnorm(x, norm_mix[l])
        if l < N_A:
            x = x + causal_multiscale_pool(h, pool_w[l], pool_scale[l])
        else:
            if l == N_A:
                hkv = rmsnorm(x, norm_kv)
                kvf = hkv @ w_kv
                k_sh = kvf[..., :D].reshape(B, S, H, Dh).transpose(0, 2, 1, 3)
                v_sh = kvf[..., D:2 * D].reshape(B, S, H, Dh).transpose(0, 2, 1, 3)
                log_f = jax.nn.log_sigmoid(kvf[..., 2 * D:].astype(jnp.float32)
                                           + b_f.astype(jnp.float32))
                fcum_sh = jnp.cumsum(log_f, axis=1).transpose(0, 2, 1)
            b = l - N_A
            q = (h @ w_q[b]).reshape(B, S, H, Dh).transpose(0, 2, 1, 3)
            o = forgetting_attention(q, k_sh, v_sh, fcum_sh)
            x = x + o.transpose(0, 2, 1, 3).reshape(B, S, D) @ w_o[b]
        h = rmsnorm(x, norm_ffn[l])
        if l % 2 == 0:
            i = l // 2
            x = x + swiglu(h, ffn_gate[i], ffn_up[i], ffn_down[i])
        else:
            i = l // 2
            x = x + moe_swiglu(h, w_router[i], exp_gate[i], exp_up[i], exp_down[i])
    return rmsnorm(x, norm_final)
```

```python
import functools

import jax
import jax.numpy as jnp
from jax import lax
from jax.experimental import pallas as pl
from jax.experimental.pallas import tpu as pltpu

F32 = jnp.float32
BF16 = jnp.bfloat16

D_MODEL = 1024
HEAD_DIM = 64
N_HEADS = D_MODEL // HEAD_DIM
HEAD_LANES = 128
POOL_WINDOWS = (2, 4, 8, 16)
POOL_GROUP = D_MODEL // len(POOL_WINDOWS)
POOL_HALO = 16
N_EXPERTS = 8
TOP_K = 2
EPS = 1e-5
NEG_INF = -1e30

FQ_LANE = HEAD_DIM
FK_LANE = HEAD_DIM + 3
N_PIECES = 3

TM_TOKENS = 512
TQ_ATTN = 512
TK_ATTN = 512
TM_EXPERT = 512
FF_CHUNK = 512

VMEM_LIMIT_BYTES = 56 * 1024 * 1024


def _params(*sem):
    return pltpu.CompilerParams(dimension_semantics=sem, vmem_limit_bytes=VMEM_LIMIT_BYTES)


def _const_spec(shape):
    zeros = (0,) * len(shape)
    return pl.BlockSpec(shape, lambda *_: zeros, pipeline_mode=pl.Buffered(1))


def _rms(v):
    return v * lax.rsqrt(jnp.mean(v * v, axis=-1, keepdims=True) + EPS)


def _split3(v):
    hi = v.astype(BF16)
    r1 = v - hi.astype(F32)
    mid = r1.astype(BF16)
    lo = (r1 - mid.astype(F32)).astype(BF16)
    return hi, mid, lo


def _lower_tri(n, strict):
    r = lax.broadcasted_iota(jnp.int32, (n, n), 0)
    c = lax.broadcasted_iota(jnp.int32, (n, n), 1)
    return jnp.where((c < r) if strict else (c <= r), 1.0, 0.0).astype(BF16)


def _layer0_kernel(x_ref, xp_ref, gmix_ref, gffn_ref, pw_ref, ps_ref, wg_ref, wu_ref, wd_ref,
                   o_ref, *, tiles_per_seq):
    tm = x_ref.shape[0]
    t = pl.program_id(0) % tiles_per_seq
    x = x_ref[...]
    gmix = gmix_ref[...]
    h = _rms(x) * gmix
    hp = jnp.where(t == 0, 0.0, _rms(xp_ref[...]) * gmix)
    ext = jnp.concatenate([hp, h], axis=0)
    pos = t * tm + lax.broadcasted_iota(jnp.int32, (tm, 1), 0)
    mixed = []
    for g, w in enumerate(POOL_WINDOWS):
        lanes = slice(g * POOL_GROUP, (g + 1) * POOL_GROUP)
        win = ext[:, lanes]
        shift = 1
        while shift < w:
            win = win + pltpu.roll(win, shift, axis=0)
            shift *= 2
        cnt = jnp.minimum(pos + 1, w).astype(F32)
        pooled = win[POOL_HALO:, :] / cnt - h[:, lanes]
        mixed.append(jnp.dot(pooled.astype(BF16), pw_ref[g], preferred_element_type=F32))
    x1 = x + jnp.concatenate(mixed, axis=1) * ps_ref[...]
    h2 = (_rms(x1) * gffn_ref[...]).astype(BF16)
    gate = jnp.dot(h2, wg_ref[...], preferred_element_type=F32)
    up = jnp.dot(h2, wu_ref[...], preferred_element_type=F32)
    act = (gate * jax.nn.sigmoid(gate) * up).astype(BF16)
    o_ref[...] = x1 + jnp.dot(act, wd_ref[...], preferred_element_type=F32)


def _layer0(x2d, seq, gmix, gffn, pool_w, pool_scale, wg, wu, wd):
    n_tok, d = x2d.shape
    tm = TM_TOKENS
    tiles_per_seq = seq // tm
    halo_blocks = tm // POOL_HALO
    d_ff = wg.shape[1]
    return pl.pallas_call(
        functools.partial(_layer0_kernel, tiles_per_seq=tiles_per_seq),
        grid=(n_tok // tm,),
        in_specs=[
            pl.BlockSpec((tm, d), lambda i: (i, 0)),
            pl.BlockSpec((POOL_HALO, d), lambda i: (jnp.maximum(i * halo_blocks - 1, 0), 0)),
            _const_spec((1, d)), _const_spec((1, d)),
            _const_spec(pool_w.shape), _const_spec((1, d)),
            _const_spec((d, d_ff)), _const_spec((d, d_ff)), _const_spec((d_ff, d)),
        ],
        out_specs=pl.BlockSpec((tm, d), lambda i: (i, 0)),
        out_shape=jax.ShapeDtypeStruct((n_tok, d), F32),
        compiler_params=_params("arbitrary"),
        name="layer0",
    )(x2d, x2d, gmix, gffn, pool_w, pool_scale, wg, wu, wd)


def _qkv_kernel(x_ref, gq_ref, gkv_ref, wq_ref, wk_ref, wv_ref, wf_ref, bf_ref, pq_ref, pk_ref,
                cq_ref, ck_ref, cv_ref, q_ref, k_ref, v_ref, carry_ref, *, tiles_per_seq):
    tm = x_ref.shape[0]

    @pl.when(pl.program_id(0) % tiles_per_seq == 0)
    def _():
        carry_ref[...] = jnp.zeros_like(carry_ref)

    xn = _rms(x_ref[...])
    hq = (xn * gq_ref[...]).astype(BF16)
    hkv = (xn * gkv_ref[...]).astype(BF16)

    z = jnp.dot(hkv, wf_ref[...], preferred_element_type=F32) + bf_ref[...]
    logf = jnp.minimum(z, 0.0) - jnp.log1p(jnp.exp(-jnp.abs(z)))
    tri = _lower_tri(tm, strict=False)
    fcum = carry_ref[...] + sum(jnp.dot(tri, p, preferred_element_type=F32) for p in _split3(logf))
    carry_ref[...] = fcum[tm - 1:tm, :]
    hi, mid, lo = _split3(fcum)
    lane = lax.broadcasted_iota(jnp.int32, fcum.shape, 1)
    pieces = jnp.where(lane < N_HEADS, hi, jnp.where(lane < 2 * N_HEADS, mid, lo))

    q = (jnp.dot(hq, wq_ref[...], preferred_element_type=F32)
         + jnp.dot(pieces, pq_ref[...], preferred_element_type=F32) + cq_ref[...]).astype(BF16)
    k = (jnp.dot(hkv, wk_ref[...], preferred_element_type=F32)
         + jnp.dot(pieces, pk_ref[...], preferred_element_type=F32) + ck_ref[...]).astype(BF16)
    v = (jnp.dot(hkv, wv_ref[...], preferred_element_type=F32) + cv_ref[...]).astype(BF16)
    for h in range(N_HEADS):
        lanes = slice(h * HEAD_LANES, (h + 1) * HEAD_LANES)
        q_ref[0, h] = q[:, lanes]
        k_ref[0, h] = k[:, lanes]
        v_ref[0, h] = v[:, lanes]


def _qkv(x2d, batch, seq, gq, gkv, wq, wk, wv, wf, bf, pq, pk, cq, ck, cv):
    n_tok, d = x2d.shape
    tm = TM_TOKENS
    tiles_per_seq = seq // tm
    wide = N_HEADS * HEAD_LANES
    head_spec = pl.BlockSpec((1, N_HEADS, tm, HEAD_LANES),
                             lambda i: (i // tiles_per_seq, 0, i % tiles_per_seq, 0))
    head_shape = jax.ShapeDtypeStruct((batch, N_HEADS, seq, HEAD_LANES), BF16)
    return pl.pallas_call(
        functools.partial(_qkv_kernel, tiles_per_seq=tiles_per_seq),
        grid=(n_tok // tm,),
        in_specs=[
            pl.BlockSpec((tm, d), lambda i: (i, 0)),
            _const_spec((1, d)), _const_spec((1, d)),
            _const_spec((d, wide)), _const_spec((d, wide)), _const_spec((d, wide)),
            _const_spec((d, HEAD_LANES)), _const_spec((1, HEAD_LANES)),
            _const_spec((HEAD_LANES, wide)), _const_spec((HEAD_LANES, wide)),
            _const_spec((1, wide)), _const_spec((1, wide)), _const_spec((1, wide)),
        ],
        out_specs=[head_spec, head_spec, head_spec],
        out_shape=[head_shape, head_shape, head_shape],
        scratch_shapes=[pltpu.VMEM((1, HEAD_LANES), F32)],
        compiler_params=_params("arbitrary"),
        name="qkv",
    )(x2d, gq, gkv, wq, wk, wv, wf, bf, pq, pk, cq, ck, cv)


def _attn_kernel(q_ref, k_ref, v_ref, o_ref, *, tk):
    tq = q_ref.shape[2]
    qi = pl.program_id(2)
    chunks_per_q = tq // tk
    n_below = qi * chunks_per_q
    row = lax.broadcasted_iota(jnp.int32, (tq, tk), 0)
    col = lax.broadcasted_iota(jnp.int32, (tq, tk), 1)
    outs = []
    for hh in range(2):
        q = q_ref[0, hh]

        def step(j, carry, masked_offset=None):
            m, acc = carry
            start = pl.multiple_of(j * tk, tk)
            k = k_ref[0, hh, pl.ds(start, tk), :]
            v = v_ref[0, hh, pl.ds(start, tk), :]
            s = lax.dot_general(q, k, (((1,), (1,)), ((), ())), preferred_element_type=F32)
            if masked_offset is not None:
                s = jnp.where(row >= col + masked_offset, s, NEG_INF)
            m_new = jnp.maximum(m, jnp.max(s, axis=-1, keepdims=True))
            p = jnp.exp(s - m_new)
            acc = acc * jnp.exp(m - m_new) + jnp.dot(p.astype(BF16), v, preferred_element_type=F32)
            return m_new, acc

        carry = (jnp.full((tq, 1), NEG_INF, F32), jnp.zeros((tq, HEAD_LANES), F32))
        carry = lax.fori_loop(0, n_below, step, carry)
        for dgl in range(chunks_per_q):
            carry = step(n_below + dgl, carry, masked_offset=dgl * tk)
        outs.append(carry[1])
    even = outs[0] / outs[0][:, HEAD_DIM:HEAD_DIM + 1]
    odd = outs[1] / outs[1][:, 0:1]
    lane = lax.broadcasted_iota(jnp.int32, even.shape, 1)
    o_ref[0] = jnp.where(lane < HEAD_DIM, even, odd).astype(o_ref.dtype)


def _attention(q, k, v):
    batch, _, seq, _ = q.shape
    tq = TQ_ATTN
    pair_kv = pl.BlockSpec((1, 2, seq, HEAD_LANES), lambda b, hp, i: (b, hp, 0, 0))
    return pl.pallas_call(
        functools.partial(_attn_kernel, tk=TK_ATTN),
        grid=(batch, N_HEADS // 2, seq // tq),
        in_specs=[pl.BlockSpec((1, 2, tq, HEAD_LANES), lambda b, hp, i: (b, hp, i, 0)), pair_kv, pair_kv],
        out_specs=pl.BlockSpec((1, tq, HEAD_LANES), lambda b, hp, i: (b, i, hp)),
        out_shape=jax.ShapeDtypeStruct((batch, seq, D_MODEL), BF16),
        compiler_params=_params("arbitrary", "arbitrary", "arbitrary"),
        name="attn",
    )(q, k, v)


META_LANES = 8


def _oproj_kernel(o_ref, x_ref, wo_ref, g_ref, wrh_ref, wrl_ref,
                  x3_ref, h3_ref, meta_ref, cnt_ref, carry_ref):
    tm = x_ref.shape[0]

    @pl.when(pl.program_id(0) == 0)
    def _():
        carry_ref[...] = jnp.zeros_like(carry_ref)

    x3 = x_ref[...] + jnp.dot(o_ref[...], wo_ref[...], preferred_element_type=F32)
    x3_ref[...] = x3
    h3 = _rms(x3) * g_ref[...]
    h_hi = h3.astype(BF16)
    h_lo = (h3 - h_hi.astype(F32)).astype(BF16)
    h3_ref[...] = h_hi
    logits = (jnp.dot(h_hi, wrh_ref[...], preferred_element_type=F32)
              + jnp.dot(h_lo, wrh_ref[...], preferred_element_type=F32)
              + jnp.dot(h_hi, wrl_ref[...], preferred_element_type=F32))
    lane = lax.broadcasted_iota(jnp.int32, logits.shape, 1)
    lg = jnp.where(lane < N_EXPERTS, logits, -jnp.inf)
    m1 = jnp.max(lg, axis=-1, keepdims=True)
    i1 = jnp.min(jnp.where(lg == m1, lane, HEAD_LANES), axis=-1, keepdims=True)
    lg2 = jnp.where(lane == i1, -jnp.inf, lg)
    m2 = jnp.max(lg2, axis=-1, keepdims=True)
    i2 = jnp.min(jnp.where(lg2 == m2, lane, HEAD_LANES), axis=-1, keepdims=True)
    e2 = jnp.exp(m2 - m1)
    w1 = 1.0 / (1.0 + e2)
    w2 = e2 / (1.0 + e2)
    sel1 = lane == i1
    sel2 = lane == i2
    chosen = jnp.where(sel1 | sel2, 1.0, 0.0)
    before = carry_ref[...] + jnp.dot(_lower_tri(tm, strict=True), chosen.astype(BF16),
                                      preferred_element_type=F32)
    carry_ref[...] = before[tm - 1:tm, :] + chosen[tm - 1:tm, :]
    cnt_ref[...] = carry_ref[...]
    r1 = jnp.sum(jnp.where(sel1, before, 0.0), axis=-1, keepdims=True)
    r2 = jnp.sum(jnp.where(sel2, before, 0.0), axis=-1, keepdims=True)
    meta = jnp.zeros(logits.shape, F32)
    for idx, val in enumerate((i1.astype(F32), i2.astype(F32), r1, r2, w1, w2)):
        meta = jnp.where(lane == idx, val, meta)
    meta_ref[...] = meta[:, :META_LANES]


def _oproj(o2d, x2d, wo, g, wr_hi, wr_lo):
    n_tok, d = x2d.shape
    tm = TM_TOKENS
    row = lambda i: (i, 0)
    return pl.pallas_call(
        _oproj_kernel,
        grid=(n_tok // tm,),
        in_specs=[
            pl.BlockSpec((tm, d), row), pl.BlockSpec((tm, d), row),
            _const_spec((d, d)), _const_spec((1, d)),
            _const_spec((d, HEAD_LANES)), _const_spec((d, HEAD_LANES)),
        ],
        out_specs=[pl.BlockSpec((tm, d), row), pl.BlockSpec((tm, d), row),
                   pl.BlockSpec((tm, META_LANES), row),
                   pl.BlockSpec((1, HEAD_LANES), lambda i: (0, 0))],
        out_shape=[jax.ShapeDtypeStruct((n_tok, d), F32), jax.ShapeDtypeStruct((n_tok, d), BF16),
                   jax.ShapeDtypeStruct((n_tok, META_LANES), F32),
                   jax.ShapeDtypeStruct((1, HEAD_LANES), F32)],
        scratch_shapes=[pltpu.VMEM((1, HEAD_LANES), F32)],
        compiler_params=_params("arbitrary"),
        name="oproj",
    )(o2d, x2d, wo, g, wr_hi, wr_lo)


def _moe_kernel(te_ref, nu_ref, h_ref, wg_ref, wu_ref, wd_ref, y_ref, acc_ref):
    i = pl.program_id(0)
    c = pl.program_id(1)
    last = pl.num_programs(1) - 1
    used = i < nu_ref[0]

    @pl.when(used)
    def _():
        h = h_ref[...]
        gate = jnp.dot(h, wg_ref[0], preferred_element_type=F32)
        up = jnp.dot(h, wu_ref[0], preferred_element_type=F32)
        act = (gate * jax.nn.sigmoid(gate) * up).astype(BF16)
        part = jnp.dot(act, wd_ref[0], preferred_element_type=F32)

        @pl.when(c == 0)
        def _():
            acc_ref[...] = part

        @pl.when(c > 0)
        def _():
            acc_ref[...] += part

        @pl.when(c == last)
        def _():
            y_ref[...] = acc_ref[...]

    @pl.when(jnp.logical_not(used) & (c == last))
    def _():
        y_ref[...] = jnp.zeros_like(y_ref)


def _moe(tile_expert, n_used, h_sorted, wg, wu, wd):
    n_slots, d = h_sorted.shape
    tm = TM_EXPERT
    fc = FF_CHUNK
    n_chunks = wg.shape[2] // fc
    def row_map(i, c, te, nu):
        return (jnp.minimum(i, nu[0] - 1), 0)
    def chunk(i, c, te, nu):
        return jnp.where(i < nu[0], c, n_chunks - 1)
    return pl.pallas_call(
        _moe_kernel,
        grid_spec=pltpu.PrefetchScalarGridSpec(
            num_scalar_prefetch=2,
            grid=(n_slots // tm, n_chunks),
            in_specs=[
                pl.BlockSpec((tm, d), row_map),
                pl.BlockSpec((1, d, fc), lambda i, c, te, nu: (te[i], 0, chunk(i, c, te, nu))),
                pl.BlockSpec((1, d, fc), lambda i, c, te, nu: (te[i], 0, chunk(i, c, te, nu))),
                pl.BlockSpec((1, fc, d), lambda i, c, te, nu: (te[i], chunk(i, c, te, nu), 0)),
            ],
            out_specs=pl.BlockSpec((tm, d), lambda i, c, te, nu: (i, 0)),
            scratch_shapes=[pltpu.VMEM((tm, d), F32)],
        ),
        out_shape=jax.ShapeDtypeStruct((n_slots, d), F32),
        compiler_params=_params("arbitrary", "arbitrary"),
        name="moe",
    )(tile_expert, n_used, h_sorted, wg, wu, wd)


def _final_kernel(x_ref, ya_ref, yb_ref, meta_ref, g_ref, o_ref):
    meta = meta_ref[...]
    x = x_ref[...] + meta[:, 4:5] * ya_ref[...] + meta[:, 5:6] * yb_ref[...]
    o_ref[...] = _rms(x) * g_ref[...]


def _final(x2d, ya, yb, meta, g):
    n_tok, d = x2d.shape
    tm = TM_TOKENS
    row = lambda i: (i, 0)
    big = pl.BlockSpec((tm, d), row)
    return pl.pallas_call(
        _final_kernel,
        grid=(n_tok // tm,),
        in_specs=[big, big, big, pl.BlockSpec((tm, META_LANES), row), _const_spec((1, d))],
        out_specs=big,
        out_shape=jax.ShapeDtypeStruct((n_tok, d), F32),
        compiler_params=_params("arbitrary"),
        name="final",
    )(x2d, ya, yb, meta, g)


def _head_layout(w, odd_shift):
    d = w.shape[0]
    w = w.reshape(d, N_HEADS, HEAD_DIM)
    lo = jnp.concatenate([w, jnp.zeros_like(w)], axis=-1)
    if odd_shift:
        hi = jnp.concatenate([jnp.zeros_like(w), w], axis=-1)
        odd = (jnp.arange(N_HEADS) % 2 == 1)[None, :, None]
        lo = jnp.where(odd, hi, lo)
    return lo.reshape(d, N_HEADS * HEAD_LANES)


def _gate_constants():
    wide = N_HEADS * HEAD_LANES
    piece_row = jnp.arange(HEAD_LANES)
    head = piece_row % N_HEADS
    piece = piece_row // N_HEADS
    valid = piece < N_PIECES
    colq = head * HEAD_LANES + FQ_LANE + piece
    colk = head * HEAD_LANES + FK_LANE + piece
    cols = jnp.arange(wide)[None, :]
    pq = jnp.where(valid[:, None] & (cols == colq[:, None]), 1.0, 0.0).astype(BF16)
    pk = jnp.where(valid[:, None] & (cols == colk[:, None]), -1.0, 0.0).astype(BF16)
    lane = jnp.arange(wide) % HEAD_LANES
    head_of = jnp.arange(wide) // HEAD_LANES
    cq = ((lane >= FK_LANE) & (lane < FK_LANE + N_PIECES)).astype(F32)[None, :]
    ck = ((lane >= FQ_LANE) & (lane < FQ_LANE + N_PIECES)).astype(F32)[None, :]
    cv = jnp.where(head_of % 2 == 0, lane == HEAD_DIM, lane == 0).astype(F32)[None, :]
    return pq, pk, cq, ck, cv


def kernel(x, norm_mix, norm_ffn, pool_w, pool_scale, norm_kv, w_kv, b_f, w_q, w_o, ffn_gate, ffn_up,
           ffn_down, w_router, exp_gate, exp_up, exp_down, norm_final):
    batch, seq, d = x.shape
    n_tok = batch * seq
    assert d == D_MODEL and seq % TM_TOKENS == 0 and seq % TQ_ATTN == 0 and TQ_ATTN % TK_ATTN == 0
    row = lambda v: v.reshape(1, -1).astype(F32)
    x2d = x.reshape(n_tok, d)

    x2 = _layer0(x2d, seq, row(norm_mix[0]), row(norm_ffn[0]), pool_w[0].astype(BF16), row(pool_scale[0]),
                 ffn_gate[0].astype(BF16), ffn_up[0].astype(BF16), ffn_down[0].astype(BF16))

    wq = _head_layout(w_q[0] * (HEAD_DIM ** -0.5), odd_shift=False).astype(BF16)
    wk = _head_layout(w_kv[:, :d], odd_shift=False).astype(BF16)
    wv = _head_layout(w_kv[:, d:2 * d], odd_shift=True).astype(BF16)
    wf = jnp.zeros((d, HEAD_LANES), F32).at[:, :N_PIECES * N_HEADS].set(
        jnp.tile(w_kv[:, 2 * d:], (1, N_PIECES))).astype(BF16)
    bf = jnp.zeros((1, HEAD_LANES), F32).at[0, :N_PIECES * N_HEADS].set(jnp.tile(b_f.astype(F32), N_PIECES))
    q, k, v = _qkv(x2, batch, seq, row(norm_mix[1]), row(norm_kv), wq, wk, wv, wf, bf, *_gate_constants())

    o = _attention(q, k, v)

    wr = jnp.zeros((d, HEAD_LANES), F32).at[:, :N_EXPERTS].set(w_router[0])
    wr_hi = wr.astype(BF16)
    wr_lo = (wr - wr_hi.astype(F32)).astype(BF16)
    x3, h3, meta, counts = _oproj(o.reshape(n_tok, d), x2, w_o[0].astype(BF16), row(norm_ffn[1]), wr_hi, wr_lo)

    tm = TM_EXPERT
    n_tiles = n_tok * TOP_K // tm + N_EXPERTS
    counts = counts[0, :N_EXPERTS].astype(jnp.int32)
    padded = (counts + tm - 1) // tm * tm
    group_end = jnp.cumsum(padded)
    group_start = group_end - padded
    expert = meta[:, 0:2].astype(jnp.int32)
    slot = group_start[expert] + meta[:, 2:4].astype(jnp.int32)
    n_used = (group_end[-1] // tm).astype(jnp.int32).reshape(1)
    tile_expert = jnp.sum(jnp.arange(n_tiles)[:, None] * tm >= group_end[None, :], axis=1)
    last_expert = jnp.max(jnp.where(counts > 0, jnp.arange(N_EXPERTS), 0))
    tile_expert = jnp.minimum(tile_expert, last_expert).astype(jnp.int32)
    src = jnp.zeros((n_tiles * tm,), jnp.int32).at[slot.reshape(-1)].set(
        jnp.repeat(jnp.arange(n_tok, dtype=jnp.int32), TOP_K))
    h_sorted = jnp.take(h3, src, axis=0)

    y = _moe(tile_expert, n_used, h_sorted, exp_gate[0].astype(BF16), exp_up[0].astype(BF16),
             exp_down[0].astype(BF16))

    ya = jnp.take(y, slot[:, 0], axis=0)
    yb = jnp.take(y, slot[:, 1], axis=0)
    out = _final(x3, ya, yb, meta, row(norm_final))
    return out.reshape(batch, seq, d)
```

```python
import functools

import jax
import jax.numpy as jnp
import numpy as np
from jax import lax
from jax.experimental import pallas as pl
from jax.experimental.pallas import tpu as pltpu

F32 = jnp.float32
BF16 = jnp.bfloat16

D_MODEL = 1024
HEAD_DIM = 64
N_HEADS = D_MODEL // HEAD_DIM
HEAD_LANES = 128
POOL_WINDOWS = (2, 4, 8, 16)
POOL_GROUP = D_MODEL // len(POOL_WINDOWS)
POOL_HALO = 16
N_EXPERTS = 8
TOP_K = 2
EPS = 1e-5
NEG_INF = -1e30

N_PIECES = 3

TM_TOKENS = 512
TQ_ATTN = 1024
TK_ATTN = 512
TM_EXPERT = 1024
FF_CHUNK = 512
FF_SUB = 256

VMEM_LIMIT_BYTES = 56 * 1024 * 1024


def _params(*sem):
    return pltpu.CompilerParams(dimension_semantics=sem, vmem_limit_bytes=VMEM_LIMIT_BYTES)


def _const_spec(shape):
    zeros = (0,) * len(shape)
    return pl.BlockSpec(shape, lambda *_: zeros, pipeline_mode=pl.Buffered(1))


def _rms(v):
    return v * lax.rsqrt(jnp.mean(v * v, axis=-1, keepdims=True) + EPS)


def _split3(v):
    hi = v.astype(BF16)
    r1 = v - hi.astype(F32)
    mid = r1.astype(BF16)
    lo = (r1 - mid.astype(F32)).astype(BF16)
    return hi, mid, lo


def _lower_tri(n, strict):
    r = lax.broadcasted_iota(jnp.int32, (n, n), 0)
    c = lax.broadcasted_iota(jnp.int32, (n, n), 1)
    return jnp.where((c < r) if strict else (c <= r), 1.0, 0.0).astype(BF16)


def _layer0_kernel(x_ref, xp_ref, gmix_ref, gffn_ref, pw_ref, ps_ref, wg_ref, wu_ref, wd_ref,
                   o_ref, *, tiles_per_seq):
    tm = x_ref.shape[0]
    t = pl.program_id(0) % tiles_per_seq
    x = x_ref[...]
    gmix = gmix_ref[...]
    h = _rms(x) * gmix
    hp = jnp.where(t == 0, 0.0, _rms(xp_ref[...]) * gmix)
    ext = jnp.concatenate([hp, h], axis=0)
    pos = t * tm + lax.broadcasted_iota(jnp.int32, (tm, 1), 0)
    mixed = []
    for g, w in enumerate(POOL_WINDOWS):
        lanes = slice(g * POOL_GROUP, (g + 1) * POOL_GROUP)
        win = ext[:, lanes]
        shift = 1
        while shift < w:
            win = win + pltpu.roll(win, shift, axis=0)
            shift *= 2
        cnt = jnp.minimum(pos + 1, w).astype(F32)
        pooled = win[POOL_HALO:, :] / cnt - h[:, lanes]
        mixed.append(jnp.dot(pooled.astype(BF16), pw_ref[g], preferred_element_type=F32))
    x1 = x + jnp.concatenate(mixed, axis=1) * ps_ref[...]
    h2 = (_rms(x1) * gffn_ref[...]).astype(BF16)
    gate = jnp.dot(h2, wg_ref[...], preferred_element_type=F32)
    up = jnp.dot(h2, wu_ref[...], preferred_element_type=F32)
    act = (gate * jax.nn.sigmoid(gate) * up).astype(BF16)
    o_ref[...] = x1 + jnp.dot(act, wd_ref[...], preferred_element_type=F32)


def _layer0(x2d, seq, gmix, gffn, pool_w, pool_scale, wg, wu, wd):
    n_tok, d = x2d.shape
    tm = TM_TOKENS
    tiles_per_seq = seq // tm
    halo_blocks = tm // POOL_HALO
    d_ff = wg.shape[1]
    return pl.pallas_call(
        functools.partial(_layer0_kernel, tiles_per_seq=tiles_per_seq),
        grid=(n_tok // tm,),
        in_specs=[
            pl.BlockSpec((tm, d), lambda i: (i, 0)),
            pl.BlockSpec((POOL_HALO, d), lambda i: (jnp.maximum(i * halo_blocks - 1, 0), 0)),
            _const_spec((1, d)), _const_spec((1, d)),
            _const_spec(pool_w.shape), _const_spec((1, d)),
            _const_spec((d, d_ff)), _const_spec((d, d_ff)), _const_spec((d_ff, d)),
        ],
        out_specs=pl.BlockSpec((tm, d), lambda i: (i, 0)),
        out_shape=jax.ShapeDtypeStruct((n_tok, d), F32),
        compiler_params=_params("arbitrary"),
        name="layer0",
    )(x2d, x2d, gmix, gffn, pool_w, pool_scale, wg, wu, wd)


def _qkv_kernel(x_ref, gq_ref, gkv_ref, wq_ref, wk_ref, wv_ref, wf_ref, bf_ref, pq_ref, pk_ref,
                cq_ref, ck_ref, cv_ref, q_ref, k_ref, v_ref, carry_ref, *, tiles_per_seq):
    tm = x_ref.shape[0]

    @pl.when(pl.program_id(0) % tiles_per_seq == 0)
    def _():
        carry_ref[...] = jnp.zeros_like(carry_ref)

    xn = _rms(x_ref[...])
    hq = (xn * gq_ref[...]).astype(BF16)
    hkv = (xn * gkv_ref[...]).astype(BF16)

    z = jnp.dot(hkv, wf_ref[...], preferred_element_type=F32) + bf_ref[...]
    logf = jnp.minimum(z, 0.0) - jnp.log1p(jnp.exp(-jnp.abs(z)))
    tri = _lower_tri(tm, strict=False)
    fcum = carry_ref[...] + sum(jnp.dot(tri, p, preferred_element_type=F32) for p in _split3(logf))
    carry_ref[...] = fcum[tm - 1:tm, :]
    hi, mid, lo = _split3(fcum)
    lane = lax.broadcasted_iota(jnp.int32, fcum.shape, 1)
    pieces = jnp.where(lane < N_HEADS, hi, jnp.where(lane < 2 * N_HEADS, mid, lo))

    q = jnp.dot(hq, wq_ref[...], preferred_element_type=F32)
    k = jnp.dot(hkv, wk_ref[...], preferred_element_type=F32)
    v = jnp.dot(hkv, wv_ref[...], preferred_element_type=F32)
    gq = jnp.dot(pieces, pq_ref[...], preferred_element_type=F32) + cq_ref[...]
    gk = jnp.dot(pieces, pk_ref[...], preferred_element_type=F32) + ck_ref[...]
    low = lax.broadcasted_iota(jnp.int32, (tm, HEAD_LANES), 1) < HEAD_DIM
    for pair in range(N_HEADS // 2):
        data = slice(pair * HEAD_LANES, (pair + 1) * HEAD_LANES)
        for odd in range(2):
            h = 2 * pair + odd
            keep = jnp.logical_not(low) if odd else low
            extra = slice(h * HEAD_LANES, (h + 1) * HEAD_LANES)
            q_ref[0, h] = jnp.where(keep, q[:, data], gq[:, extra]).astype(BF16)
            k_ref[0, h] = jnp.where(keep, k[:, data], gk[:, extra]).astype(BF16)
            v_ref[0, h] = jnp.where(keep, v[:, data], cv_ref[:, extra]).astype(BF16)


def _qkv(x2d, batch, seq, gq, gkv, wq, wk, wv, wf, bf, pq, pk, cq, ck, cv):
    n_tok, d = x2d.shape
    tm = TM_TOKENS
    tiles_per_seq = seq // tm
    wide = N_HEADS * HEAD_LANES
    head_spec = pl.BlockSpec((1, N_HEADS, tm, HEAD_LANES),
                             lambda i: (i // tiles_per_seq, 0, i % tiles_per_seq, 0))
    head_shape = jax.ShapeDtypeStruct((batch, N_HEADS, seq, HEAD_LANES), BF16)
    return pl.pallas_call(
        functools.partial(_qkv_kernel, tiles_per_seq=tiles_per_seq),
        grid=(n_tok // tm,),
        in_specs=[
            pl.BlockSpec((tm, d), lambda i: (i, 0)),
            _const_spec((1, d)), _const_spec((1, d)),
            _const_spec((d, d)), _const_spec((d, d)), _const_spec((d, d)),
            _const_spec((d, HEAD_LANES)), _const_spec((1, HEAD_LANES)),
            _const_spec((HEAD_LANES, wide)), _const_spec((HEAD_LANES, wide)),
            _const_spec((1, wide)), _const_spec((1, wide)), _const_spec((1, wide)),
        ],
        out_specs=[head_spec, head_spec, head_spec],
        out_shape=[head_shape, head_shape, head_shape],
        scratch_shapes=[pltpu.VMEM((1, HEAD_LANES), F32)],
        compiler_params=_params("arbitrary"),
        name="qkv",
    )(x2d, gq, gkv, wq, wk, wv, wf, bf, pq, pk, cq, ck, cv)


def _attn_kernel(q_ref, k_ref, v_ref, o_ref, s_buf, p_buf, corr_buf, m_buf, acc_buf, *, tk):
    tq = q_ref.shape[2]
    qi = pl.program_id(2)
    chunks_per_q = tq // tk
    n_below = qi * chunks_per_q
    heads = range(2)

    def scores(hh, j, rows=slice(None)):
        k = k_ref[0, hh, pl.ds(pl.multiple_of(j * tk, tk), tk), :]
        return lax.dot_general(q_ref[0, hh, rows, :], k, (((1,), (1,)), ((), ())),
                               preferred_element_type=F32)

    def softmax_chunk(s, m):
        blocks = [s[:, c:c + HEAD_LANES] for c in range(0, s.shape[1], HEAD_LANES)]
        m_new = jnp.maximum(m, jnp.max(functools.reduce(jnp.maximum, blocks), axis=-1, keepdims=True))
        p = jnp.concatenate([jnp.exp(b - m_new) for b in blocks], axis=1)
        return m_new, p.astype(BF16), jnp.exp(m - m_new)

    def value_matmul(hh, j, p):
        v = v_ref[0, hh, pl.ds(pl.multiple_of(j * tk, tk), tk), :]
        return jnp.dot(p, v, preferred_element_type=F32)

    def step(j, slot):
        other = 1 - slot
        for hh in heads:
            s_buf[hh, other] = scores(hh, j + 1)
            acc_buf[hh] = (acc_buf[hh] * corr_buf[hh, other]
                           + value_matmul(hh, jnp.maximum(j - 1, 0), p_buf[hh, other]))
            m, p, corr = softmax_chunk(s_buf[hh, slot], m_buf[hh])
            m_buf[hh] = m
            p_buf[hh, slot] = p
            corr_buf[hh, slot] = corr

    for hh in heads:
        s_buf[hh, 0] = scores(hh, 0)
        p_buf[hh, 1] = jnp.zeros(p_buf.shape[2:], BF16)
        corr_buf[hh, 1] = jnp.ones(corr_buf.shape[2:], F32)
        m_buf[hh] = jnp.full(m_buf.shape[1:], NEG_INF, F32)
        acc_buf[hh] = jnp.zeros(acc_buf.shape[1:], F32)

    def pair(jj, _):
        step(2 * jj, 0)
        step(2 * jj + 1, 1)
        return 0

    lax.fori_loop(0, n_below // 2, pair, 0)

    outs = []
    tri = (lax.broadcasted_iota(jnp.int32, (tk, tk), 0) >= lax.broadcasted_iota(jnp.int32, (tk, tk), 1))
    for hh in heads:
        acc = acc_buf[hh] * corr_buf[hh, 1] + value_matmul(hh, jnp.maximum(n_below - 1, 0), p_buf[hh, 1])
        m = m_buf[hh]
        for dgl in range(chunks_per_q):
            r0 = dgl * tk
            s = s_buf[hh, 0] if dgl == 0 else scores(hh, n_below + dgl, rows=slice(r0, tq))
            masked = jnp.where(tri, s[:tk], NEG_INF)
            s = masked if s.shape[0] == tk else jnp.concatenate([masked, s[tk:]], axis=0)
            m_new, p, corr = softmax_chunk(s, m[r0:])
            upd = acc[r0:] * corr + value_matmul(hh, n_below + dgl, p)
            m = m_new if r0 == 0 else jnp.concatenate([m[:r0], m_new], axis=0)
            acc = upd if r0 == 0 else jnp.concatenate([acc[:r0], upd], axis=0)
        outs.append(acc)
    even = outs[0] / outs[0][:, HEAD_DIM:HEAD_DIM + 1]
    odd = outs[1] / outs[1][:, 0:1]
    lane = lax.broadcasted_iota(jnp.int32, even.shape, 1)
    o_ref[0] = jnp.where(lane < HEAD_DIM, even, odd).astype(o_ref.dtype)


def _attention(q, k, v):
    batch, _, seq, _ = q.shape
    tq, tk = TQ_ATTN, TK_ATTN
    pair_kv = pl.BlockSpec((1, 2, seq, HEAD_LANES), lambda b, hp, i: (b, hp, 0, 0))
    return pl.pallas_call(
        functools.partial(_attn_kernel, tk=tk),
        grid=(batch, N_HEADS // 2, seq // tq),
        in_specs=[pl.BlockSpec((1, 2, tq, HEAD_LANES), lambda b, hp, i: (b, hp, i, 0)), pair_kv, pair_kv],
        out_specs=pl.BlockSpec((1, tq, HEAD_LANES), lambda b, hp, i: (b, i, hp)),
        out_shape=jax.ShapeDtypeStruct((batch, seq, D_MODEL), BF16),
        scratch_shapes=[pltpu.VMEM((2, 2, tq, tk), F32),
                        pltpu.VMEM((2, 2, tq, tk), BF16),
                        pltpu.VMEM((2, 2, tq, HEAD_LANES), F32),
                        pltpu.VMEM((2, tq, HEAD_LANES), F32),
                        pltpu.VMEM((2, tq, HEAD_LANES), F32)],
        compiler_params=_params("arbitrary", "arbitrary", "arbitrary"),
        name="attn",
    )(q, k, v)


META_LANES = 8


def _oproj_kernel(o_ref, x_ref, wo_ref, g_ref, wrh_ref, wrl_ref,
                  x3_ref, h3_ref, meta_ref, cnt_ref, carry_ref):
    tm = x_ref.shape[0]

    @pl.when(pl.program_id(0) == 0)
    def _():
        carry_ref[...] = jnp.zeros_like(carry_ref)

    x3 = x_ref[...] + jnp.dot(o_ref[...], wo_ref[...], preferred_element_type=F32)
    x3_ref[...] = x3
    h3 = _rms(x3) * g_ref[...]
    h_hi = h3.astype(BF16)
    h_lo = (h3 - h_hi.astype(F32)).astype(BF16)
    h3_ref[...] = h_hi
    logits = (jnp.dot(h_hi, wrh_ref[...], preferred_element_type=F32)
              + jnp.dot(h_lo, wrh_ref[...], preferred_element_type=F32)
              + jnp.dot(h_hi, wrl_ref[...], preferred_element_type=F32))
    lane = lax.broadcasted_iota(jnp.int32, logits.shape, 1)
    lg = jnp.where(lane < N_EXPERTS, logits, -jnp.inf)
    m1 = jnp.max(lg, axis=-1, keepdims=True)
    i1 = jnp.min(jnp.where(lg == m1, lane, HEAD_LANES), axis=-1, keepdims=True)
    lg2 = jnp.where(lane == i1, -jnp.inf, lg)
    m2 = jnp.max(lg2, axis=-1, keepdims=True)
    i2 = jnp.min(jnp.where(lg2 == m2, lane, HEAD_LANES), axis=-1, keepdims=True)
    e2 = jnp.exp(m2 - m1)
    w1 = 1.0 / (1.0 + e2)
    w2 = e2 / (1.0 + e2)
    sel1 = lane == i1
    sel2 = lane == i2
    chosen = jnp.where(sel1 | sel2, 1.0, 0.0)
    before = carry_ref[...] + jnp.dot(_lower_tri(tm, strict=True), chosen.astype(BF16),
                                      preferred_element_type=F32)
    carry_ref[...] = before[tm - 1:tm, :] + chosen[tm - 1:tm, :]
    cnt_ref[...] = carry_ref[...]
    r1 = jnp.sum(jnp.where(sel1, before, 0.0), axis=-1, keepdims=True)
    r2 = jnp.sum(jnp.where(sel2, before, 0.0), axis=-1, keepdims=True)
    meta = jnp.zeros(logits.shape, F32)
    for idx, val in enumerate((i1.astype(F32), i2.astype(F32), r1, r2, w1, w2)):
        meta = jnp.where(lane == idx, val, meta)
    meta_ref[...] = meta[:, :META_LANES]


def _oproj(o2d, x2d, wo, g, wr_hi, wr_lo):
    n_tok, d = x2d.shape
    tm = TM_TOKENS
    row = lambda i: (i, 0)
    return pl.pallas_call(
        _oproj_kernel,
        grid=(n_tok // tm,),
        in_specs=[
            pl.BlockSpec((tm, d), row), pl.BlockSpec((tm, d), row),
            _const_spec((d, d)), _const_spec((1, d)),
            _const_spec((d, HEAD_LANES)), _const_spec((d, HEAD_LANES)),
        ],
        out_specs=[pl.BlockSpec((tm, d), row), pl.BlockSpec((tm, d), row),
                   pl.BlockSpec((tm, META_LANES), row),
                   pl.BlockSpec((1, HEAD_LANES), lambda i: (0, 0))],
        out_shape=[jax.ShapeDtypeStruct((n_tok, d), F32), jax.ShapeDtypeStruct((n_tok, d), BF16),
                   jax.ShapeDtypeStruct((n_tok, META_LANES), F32),
                   jax.ShapeDtypeStruct((1, HEAD_LANES), F32)],
        scratch_shapes=[pltpu.VMEM((1, HEAD_LANES), F32)],
        compiler_params=_params("arbitrary"),
        name="oproj",
    )(o2d, x2d, wo, g, wr_hi, wr_lo)


def _moe_kernel(te_ref, nu_ref, h_ref, wg_ref, wu_ref, wd_ref, y_ref):
    @pl.when(pl.program_id(1) == 0)
    def _():
        y_ref[...] = jnp.zeros_like(y_ref)

    @pl.when(pl.program_id(0) < nu_ref[0])
    def _():
        h = h_ref[...]
        for s0 in range(0, wg_ref.shape[2], FF_SUB):
            sub = slice(s0, s0 + FF_SUB)
            gate = jnp.dot(h, wg_ref[0, :, sub].astype(BF16), preferred_element_type=F32)
            up = jnp.dot(h, wu_ref[0, :, sub].astype(BF16), preferred_element_type=F32)
            act = (gate * jax.nn.sigmoid(gate) * up).astype(BF16)
            y_ref[...] += jnp.dot(act, wd_ref[0, sub, :].astype(BF16), preferred_element_type=F32)


def _moe(tile_expert, n_used, h_sorted, wg, wu, wd):
    n_slots, d = h_sorted.shape
    tm = TM_EXPERT
    fc = FF_CHUNK
    n_chunks = wg.shape[2] // fc
    def row_map(i, c, te, nu):
        return (jnp.minimum(i, nu[0] - 1), 0)
    def chunk(i, c, te, nu):
        return jnp.where(i < nu[0], c, n_chunks - 1)
    return pl.pallas_call(
        _moe_kernel,
        grid_spec=pltpu.PrefetchScalarGridSpec(
            num_scalar_prefetch=2,
            grid=(n_slots // tm, n_chunks),
            in_specs=[
                pl.BlockSpec((tm, d), row_map),
                pl.BlockSpec((1, d, fc), lambda i, c, te, nu: (te[i], 0, chunk(i, c, te, nu))),
                pl.BlockSpec((1, d, fc), lambda i, c, te, nu: (te[i], 0, chunk(i, c, te, nu))),
                pl.BlockSpec((1, fc, d), lambda i, c, te, nu: (te[i], chunk(i, c, te, nu), 0)),
            ],
            out_specs=pl.BlockSpec((tm, d), lambda i, c, te, nu: (i, 0)),
        ),
        out_shape=jax.ShapeDtypeStruct((n_slots, d), F32),
        compiler_params=_params("arbitrary", "arbitrary"),
        name="moe",
    )(tile_expert, n_used, h_sorted, wg, wu, wd)


def _final_kernel(x_ref, ya_ref, yb_ref, meta_ref, g_ref, o_ref):
    meta = meta_ref[...]
    x = x_ref[...] + meta[:, 4:5] * ya_ref[...] + meta[:, 5:6] * yb_ref[...]
    o_ref[...] = _rms(x) * g_ref[...]


def _final(x2d, ya, yb, meta, g):
    n_tok, d = x2d.shape
    tm = TM_TOKENS
    row = lambda i: (i, 0)
    big = pl.BlockSpec((tm, d), row)
    return pl.pallas_call(
        _final_kernel,
        grid=(n_tok // tm,),
        in_specs=[big, big, big, pl.BlockSpec((tm, META_LANES), row), _const_spec((1, d))],
        out_specs=big,
        out_shape=jax.ShapeDtypeStruct((n_tok, d), F32),
        compiler_params=_params("arbitrary"),
        name="final",
    )(x2d, ya, yb, meta, g)


def _gate_constants():
    wide = N_HEADS * HEAD_LANES
    pq = np.zeros((HEAD_LANES, wide), np.float32)
    pk = np.zeros((HEAD_LANES, wide), np.float32)
    cq = np.zeros((1, wide), np.float32)
    ck = np.zeros((1, wide), np.float32)
    cv = np.zeros((1, wide), np.float32)
    for h in range(N_HEADS):
        base = h * HEAD_LANES + (HEAD_DIM if h % 2 == 0 else 0)
        cv[0, base] = 1.0
        for piece in range(N_PIECES):
            r = piece * N_HEADS + h
            pq[r, base + piece] = 1.0
            ck[0, base + piece] = 1.0
            cq[0, base + N_PIECES + piece] = 1.0
            pk[r, base + N_PIECES + piece] = -1.0
    return (jnp.asarray(pq, BF16), jnp.asarray(pk, BF16), jnp.asarray(cq), jnp.asarray(ck), jnp.asarray(cv))


def kernel(x, norm_mix, norm_ffn, pool_w, pool_scale, norm_kv, w_kv, b_f, w_q, w_o, ffn_gate, ffn_up,
           ffn_down, w_router, exp_gate, exp_up, exp_down, norm_final):
    batch, seq, d = x.shape
    n_tok = batch * seq
    assert d == D_MODEL and seq % TM_TOKENS == 0 and seq % TQ_ATTN == 0 and TQ_ATTN % (2 * TK_ATTN) == 0
    row = lambda v: v.reshape(1, -1).astype(F32)
    x2d = x.reshape(n_tok, d)

    x2 = _layer0(x2d, seq, row(norm_mix[0]), row(norm_ffn[0]), pool_w[0].astype(BF16), row(pool_scale[0]),
                 ffn_gate[0].astype(BF16), ffn_up[0].astype(BF16), ffn_down[0].astype(BF16))

    wq = (w_q[0] * (HEAD_DIM ** -0.5)).astype(BF16)
    wk = w_kv[:, :d].astype(BF16)
    wv = w_kv[:, d:2 * d].astype(BF16)
    wf = jnp.zeros((d, HEAD_LANES), F32).at[:, :N_PIECES * N_HEADS].set(
        jnp.tile(w_kv[:, 2 * d:], (1, N_PIECES))).astype(BF16)
    bf = jnp.zeros((1, HEAD_LANES), F32).at[0, :N_PIECES * N_HEADS].set(jnp.tile(b_f.astype(F32), N_PIECES))
    q, k, v = _qkv(x2, batch, seq, row(norm_mix[1]), row(norm_kv), wq, wk, wv, wf, bf, *_gate_constants())

    o = _attention(q, k, v)

    wr = jnp.zeros((d, HEAD_LANES), F32).at[:, :N_EXPERTS].set(w_router[0])
    wr_hi = wr.astype(BF16)
    wr_lo = (wr - wr_hi.astype(F32)).astype(BF16)
    x3, h3, meta, counts = _oproj(o.reshape(n_tok, d), x2, w_o[0].astype(BF16), row(norm_ffn[1]), wr_hi, wr_lo)

    tm = TM_EXPERT
    n_tiles = n_tok * TOP_K // tm + N_EXPERTS
    counts = counts[0, :N_EXPERTS].astype(jnp.int32)
    padded = (counts + tm - 1) // tm * tm
    group_end = jnp.cumsum(padded)
    group_start = group_end - padded
    expert = meta[:, 0:2].astype(jnp.int32)
    slot = group_start[expert] + meta[:, 2:4].astype(jnp.int32)
    n_used = (group_end[-1] // tm).astype(jnp.int32).reshape(1)
    tile_expert = jnp.sum(jnp.arange(n_tiles)[:, None] * tm >= group_end[None, :], axis=1)
    last_expert = jnp.max(jnp.where(counts > 0, jnp.arange(N_EXPERTS), 0))
    tile_expert = jnp.minimum(tile_expert, last_expert).astype(jnp.int32)
    src = jnp.zeros((n_tiles * tm,), jnp.int32).at[slot.reshape(-1)].set(
        jnp.repeat(jnp.arange(n_tok, dtype=jnp.int32), TOP_K))
    h_sorted = jnp.take(h3, src, axis=0)

    y = _moe(tile_expert, n_used, h_sorted, exp_gate[0], exp_up[0], exp_down[0])

    ya = jnp.take(y, slot[:, 0], axis=0)
    yb = jnp.take(y, slot[:, 1], axis=0)
    out = _final(x3, ya, yb, meta, row(norm_final))
    return out.reshape(batch, seq, d)
```

```python
import functools

import jax
import jax.numpy as jnp
import numpy as np
from jax import lax
from jax.experimental import pallas as pl
from jax.experimental.pallas import tpu as pltpu

F32 = jnp.float32
BF16 = jnp.bfloat16

D_MODEL = 1024
HEAD_DIM = 64
N_HEADS = D_MODEL // HEAD_DIM
HEAD_LANES = 128
POOL_WINDOWS = (2, 4, 8, 16)
POOL_GROUP = D_MODEL // len(POOL_WINDOWS)
POOL_HALO = 16
N_EXPERTS = 8
TOP_K = 2
EPS = 1e-5
NEG_INF = -1e30
LOG2E = 1.4426950408889634

N_PIECES = 3

TM_TOKENS = 512
TQ_ATTN = 1024
TK_ATTN = 512
TM_EXPERT = 1024
FF_CHUNK = 512
FF_SUB = 256

VMEM_LIMIT_BYTES = 56 * 1024 * 1024


def _params(*sem, flags=None):
    return pltpu.CompilerParams(dimension_semantics=sem, vmem_limit_bytes=VMEM_LIMIT_BYTES, flags=flags)


def _const_spec(shape):
    zeros = (0,) * len(shape)
    return pl.BlockSpec(shape, lambda *_: zeros, pipeline_mode=pl.Buffered(1))


def _rms(v):
    return v * lax.rsqrt(jnp.mean(v * v, axis=-1, keepdims=True) + EPS)


def _split3(v):
    hi = v.astype(BF16)
    r1 = v - hi.astype(F32)
    mid = r1.astype(BF16)
    lo = (r1 - mid.astype(F32)).astype(BF16)
    return hi, mid, lo


def _lower_tri(n, strict):
    r = lax.broadcasted_iota(jnp.int32, (n, n), 0)
    c = lax.broadcasted_iota(jnp.int32, (n, n), 1)
    return jnp.where((c < r) if strict else (c <= r), 1.0, 0.0).astype(BF16)


def _layer0_kernel(x_ref, xp_ref, gmix_ref, gffn_ref, pw_ref, ps_ref, wg_ref, wu_ref, wd_ref,
                   o_ref, *, tiles_per_seq):
    tm = x_ref.shape[0]
    t = pl.program_id(0) % tiles_per_seq
    x = x_ref[...]
    gmix = gmix_ref[...]
    h = _rms(x) * gmix
    hp = jnp.where(t == 0, 0.0, _rms(xp_ref[...]) * gmix)
    ext = jnp.concatenate([hp, h], axis=0)
    pos = t * tm + lax.broadcasted_iota(jnp.int32, (tm, 1), 0)
    mixed = []
    for g, w in enumerate(POOL_WINDOWS):
        lanes = slice(g * POOL_GROUP, (g + 1) * POOL_GROUP)
        win = ext[:, lanes]
        shift = 1
        while shift < w:
            win = win + pltpu.roll(win, shift, axis=0)
            shift *= 2
        cnt = jnp.minimum(pos + 1, w).astype(F32)
        pooled = win[POOL_HALO:, :] / cnt - h[:, lanes]
        mixed.append(jnp.dot(pooled.astype(BF16), pw_ref[g], preferred_element_type=F32))
    x1 = x + jnp.concatenate(mixed, axis=1) * ps_ref[...]
    h2 = (_rms(x1) * gffn_ref[...]).astype(BF16)
    gate = jnp.dot(h2, wg_ref[...], preferred_element_type=F32)
    up = jnp.dot(h2, wu_ref[...], preferred_element_type=F32)
    act = (gate * jax.nn.sigmoid(gate) * up).astype(BF16)
    o_ref[...] = x1 + jnp.dot(act, wd_ref[...], preferred_element_type=F32)


def _layer0(x2d, seq, gmix, gffn, pool_w, pool_scale, wg, wu, wd):
    n_tok, d = x2d.shape
    tm = TM_TOKENS
    tiles_per_seq = seq // tm
    halo_blocks = tm // POOL_HALO
    d_ff = wg.shape[1]
    return pl.pallas_call(
        functools.partial(_layer0_kernel, tiles_per_seq=tiles_per_seq),
        grid=(n_tok // tm,),
        in_specs=[
            pl.BlockSpec((tm, d), lambda i: (i, 0)),
            pl.BlockSpec((POOL_HALO, d), lambda i: (jnp.maximum(i * halo_blocks - 1, 0), 0)),
            _const_spec((1, d)), _const_spec((1, d)),
            _const_spec(pool_w.shape), _const_spec((1, d)),
            _const_spec((d, d_ff)), _const_spec((d, d_ff)), _const_spec((d_ff, d)),
        ],
        out_specs=pl.BlockSpec((tm, d), lambda i: (i, 0)),
        out_shape=jax.ShapeDtypeStruct((n_tok, d), F32),
        compiler_params=_params("arbitrary"),
        name="layer0",
    )(x2d, x2d, gmix, gffn, pool_w, pool_scale, wg, wu, wd)


def _qkv_kernel(x_ref, gq_ref, gkv_ref, wq_ref, wk_ref, wv_ref, wf_ref, bf_ref, pq_ref, pk_ref,
                cq_ref, ck_ref, cv_ref, q_ref, k_ref, v_ref, carry_ref, *, tiles_per_seq):
    tm = x_ref.shape[0]

    @pl.when(pl.program_id(0) % tiles_per_seq == 0)
    def _():
        carry_ref[...] = jnp.zeros_like(carry_ref)

    xn = _rms(x_ref[...])
    hq = (xn * gq_ref[...]).astype(BF16)
    hkv = (xn * gkv_ref[...]).astype(BF16)

    z = jnp.dot(hkv, wf_ref[...], preferred_element_type=F32) + bf_ref[...]
    logf = jnp.minimum(z, 0.0) - jnp.log1p(jnp.exp(-jnp.abs(z)))
    tri = _lower_tri(tm, strict=False)
    fcum = carry_ref[...] + sum(jnp.dot(tri, p, preferred_element_type=F32) for p in _split3(logf))
    carry_ref[...] = fcum[tm - 1:tm, :]
    hi, mid, lo = _split3(fcum * LOG2E)
    lane = lax.broadcasted_iota(jnp.int32, fcum.shape, 1)
    pieces = jnp.where(lane < N_HEADS, hi, jnp.where(lane < 2 * N_HEADS, mid, lo))

    q = jnp.dot(hq, wq_ref[...], preferred_element_type=F32) * (HEAD_DIM ** -0.5 * LOG2E)
    k = jnp.dot(hkv, wk_ref[...], preferred_element_type=F32)
    v = jnp.dot(hkv, wv_ref[...], preferred_element_type=F32)
    gq = jnp.dot(pieces, pq_ref[...], preferred_element_type=F32) + cq_ref[...]
    gk = jnp.dot(pieces, pk_ref[...], preferred_element_type=F32) + ck_ref[...]
    low = lax.broadcasted_iota(jnp.int32, (tm, HEAD_LANES), 1) < HEAD_DIM
    for pair in range(N_HEADS // 2):
        data = slice(pair * HEAD_LANES, (pair + 1) * HEAD_LANES)
        for odd in range(2):
            h = 2 * pair + odd
            keep = jnp.logical_not(low) if odd else low
            extra = slice(h * HEAD_LANES, (h + 1) * HEAD_LANES)
            q_ref[0, h] = jnp.where(keep, q[:, data], gq[:, extra]).astype(BF16)
            k_ref[0, h] = jnp.where(keep, k[:, data], gk[:, extra]).astype(BF16)
            v_ref[0, h] = jnp.where(keep, v[:, data], cv_ref[:, extra]).astype(BF16)


def _qkv(x2d, batch, seq, gq, gkv, wq, wk, wv, wf, bf, pq, pk, cq, ck, cv):
    n_tok, d = x2d.shape
    tm = TM_TOKENS
    tiles_per_seq = seq // tm
    wide = N_HEADS * HEAD_LANES
    head_spec = pl.BlockSpec((1, N_HEADS, tm, HEAD_LANES),
                             lambda i: (i // tiles_per_seq, 0, i % tiles_per_seq, 0))
    head_shape = jax.ShapeDtypeStruct((batch, N_HEADS, seq, HEAD_LANES), BF16)
    return pl.pallas_call(
        functools.partial(_qkv_kernel, tiles_per_seq=tiles_per_seq),
        grid=(n_tok // tm,),
        in_specs=[
            pl.BlockSpec((tm, d), lambda i: (i, 0)),
            _const_spec((1, d)), _const_spec((1, d)),
            _const_spec((d, d)), _const_spec((d, d)), _const_spec((d, d)),
            _const_spec((d, HEAD_LANES)), _const_spec((1, HEAD_LANES)),
            _const_spec((HEAD_LANES, wide)), _const_spec((HEAD_LANES, wide)),
            _const_spec((1, wide)), _const_spec((1, wide)), _const_spec((1, wide)),
        ],
        out_specs=[head_spec, head_spec, head_spec],
        out_shape=[head_shape, head_shape, head_shape],
        scratch_shapes=[pltpu.VMEM((1, HEAD_LANES), F32)],
        compiler_params=_params("arbitrary"),
        name="qkv",
    )(x2d, gq, gkv, wq, wk, wv, wf, bf, pq, pk, cq, ck, cv)


def _attn_kernel(q_ref, k_ref, v_ref, o_ref, p_buf, corr_buf, m_buf, acc_buf, *, tk):
    tq = q_ref.shape[2]
    qi = pl.program_id(2)
    chunks_per_q = tq // tk
    n_below = qi * chunks_per_q
    heads = range(2)
    tri = (lax.broadcasted_iota(jnp.int32, (tk, tk), 0) >= lax.broadcasted_iota(jnp.int32, (tk, tk), 1))

    def scores(hh, j, rows=slice(None)):
        k = k_ref[0, hh, pl.ds(pl.multiple_of(j * tk, tk), tk), :]
        return lax.dot_general(q_ref[0, hh, rows, :], k, (((1,), (1,)), ((), ())),
                               preferred_element_type=F32)

    def mask_top(s, keep_all):
        masked = jnp.where(tri | keep_all, s[:tk], NEG_INF)
        return masked if s.shape[0] == tk else jnp.concatenate([masked, s[tk:]], axis=0)

    def softmax_chunk(s, m):
        blocks = [s[:, c:c + HEAD_LANES] for c in range(0, s.shape[1], HEAD_LANES)]
        m_new = jnp.maximum(m, jnp.max(functools.reduce(jnp.maximum, blocks), axis=-1, keepdims=True))
        p = jnp.concatenate([jnp.exp2(b - m_new) for b in blocks], axis=1)
        return m_new, p.astype(BF16), jnp.exp2(m - m_new)

    def value_matmul(hh, j, p):
        v = v_ref[0, hh, pl.ds(pl.multiple_of(j * tk, tk), tk), :]
        return jnp.dot(p, v, preferred_element_type=F32)

    def prepare(hh, j, slot):
        m, p, corr = softmax_chunk(mask_top(scores(hh, j), j < n_below), m_buf[hh])
        m_buf[hh] = m
        p_buf[hh, slot] = p
        corr_buf[hh, slot] = corr

    def consume(hh, j, slot):
        acc_buf[hh] = acc_buf[hh] * corr_buf[hh, slot] + value_matmul(hh, j, p_buf[hh, slot])

    for hh in heads:
        m_buf[hh] = jnp.full(m_buf.shape[1:], NEG_INF, F32)
        acc_buf[hh] = jnp.zeros(acc_buf.shape[1:], F32)
        prepare(hh, 0, 0)

    def pair(jj, _):
        for slot in range(2):
            for hh in heads:
                prepare(hh, 2 * jj + slot + 1, 1 - slot)
                consume(hh, 2 * jj + slot, slot)
        return 0

    lax.fori_loop(0, n_below // 2, pair, 0)

    outs = []
    for hh in heads:
        acc = acc_buf[hh] * corr_buf[hh, 0] + value_matmul(hh, n_below, p_buf[hh, 0])
        m = m_buf[hh]
        for dgl in range(1, chunks_per_q):
            r0 = dgl * tk
            s = mask_top(scores(hh, n_below + dgl, rows=slice(r0, tq)), False)
            m_new, p, corr = softmax_chunk(s, m[r0:])
            upd = acc[r0:] * corr + value_matmul(hh, n_below + dgl, p)
            m = jnp.concatenate([m[:r0], m_new], axis=0)
            acc = jnp.concatenate([acc[:r0], upd], axis=0)
        outs.append(acc)
    even = outs[0] / outs[0][:, HEAD_DIM:HEAD_DIM + 1]
    odd = outs[1] / outs[1][:, 0:1]
    lane = lax.broadcasted_iota(jnp.int32, even.shape, 1)
    o_ref[0] = jnp.where(lane < HEAD_DIM, even, odd).astype(o_ref.dtype)


def _attention(q, k, v):
    batch, _, seq, _ = q.shape
    tq, tk = TQ_ATTN, TK_ATTN
    pair_kv = pl.BlockSpec((1, 2, seq, HEAD_LANES), lambda b, hp, i: (b, hp, 0, 0))
    return pl.pallas_call(
        functools.partial(_attn_kernel, tk=tk),
        grid=(batch, N_HEADS // 2, seq // tq),
        in_specs=[pl.BlockSpec((1, 2, tq, HEAD_LANES), lambda b, hp, i: (b, hp, i, 0)), pair_kv, pair_kv],
        out_specs=pl.BlockSpec((1, tq, HEAD_LANES), lambda b, hp, i: (b, i, hp)),
        out_shape=jax.ShapeDtypeStruct((batch, seq, D_MODEL), BF16),
        scratch_shapes=[pltpu.VMEM((2, 2, tq, tk), BF16),
                        pltpu.VMEM((2, 2, tq, HEAD_LANES), F32),
                        pltpu.VMEM((2, tq, HEAD_LANES), F32),
                        pltpu.VMEM((2, tq, HEAD_LANES), F32)],
        compiler_params=_params("arbitrary", "arbitrary", "arbitrary"),
        name="attn",
    )(q, k, v)


META_LANES = 8


def _oproj_kernel(o_ref, x_ref, wo_ref, g_ref, wrh_ref, wrl_ref,
                  x3_ref, h3_ref, meta_ref, cnt_ref, carry_ref):
    tm = x_ref.shape[0]

    @pl.when(pl.program_id(0) == 0)
    def _():
        carry_ref[...] = jnp.zeros_like(carry_ref)

    x3 = x_ref[...] + jnp.dot(o_ref[...], wo_ref[...], preferred_element_type=F32)
    x3_ref[...] = x3
    h3 = _rms(x3) * g_ref[...]
    h_hi = h3.astype(BF16)
    h_lo = (h3 - h_hi.astype(F32)).astype(BF16)
    h3_ref[...] = h3
    logits = (jnp.dot(h_hi, wrh_ref[...], preferred_element_type=F32)
              + jnp.dot(h_lo, wrh_ref[...], preferred_element_type=F32)
              + jnp.dot(h_hi, wrl_ref[...], preferred_element_type=F32))
    lane = lax.broadcasted_iota(jnp.int32, logits.shape, 1)
    lg = jnp.where(lane < N_EXPERTS, logits, -jnp.inf)
    m1 = jnp.max(lg, axis=-1, keepdims=True)
    i1 = jnp.min(jnp.where(lg == m1, lane, HEAD_LANES), axis=-1, keepdims=True)
    lg2 = jnp.where(lane == i1, -jnp.inf, lg)
    m2 = jnp.max(lg2, axis=-1, keepdims=True)
    i2 = jnp.min(jnp.where(lg2 == m2, lane, HEAD_LANES), axis=-1, keepdims=True)
    e2 = jnp.exp(m2 - m1)
    w1 = 1.0 / (1.0 + e2)
    w2 = e2 / (1.0 + e2)
    sel1 = lane == i1
    sel2 = lane == i2
    chosen = jnp.where(sel1 | sel2, 1.0, 0.0)
    before = carry_ref[...] + jnp.dot(_lower_tri(tm, strict=True), chosen.astype(BF16),
                                      preferred_element_type=F32)
    carry_ref[...] = before[tm - 1:tm, :] + chosen[tm - 1:tm, :]
    cnt_ref[...] = carry_ref[...]
    r1 = jnp.sum(jnp.where(sel1, before, 0.0), axis=-1, keepdims=True)
    r2 = jnp.sum(jnp.where(sel2, before, 0.0), axis=-1, keepdims=True)
    meta = jnp.zeros(logits.shape, F32)
    for idx, val in enumerate((i1.astype(F32), i2.astype(F32), r1, r2, w1, w2)):
        meta = jnp.where(lane == idx, val, meta)
    meta_ref[...] = meta[:, :META_LANES]


def _oproj(o2d, x2d, wo, g, wr_hi, wr_lo):
    n_tok, d = x2d.shape
    tm = TM_TOKENS
    row = lambda i: (i, 0)
    return pl.pallas_call(
        _oproj_kernel,
        grid=(n_tok // tm,),
        in_specs=[
            pl.BlockSpec((tm, d), row), pl.BlockSpec((tm, d), row),
            _const_spec((d, d)), _const_spec((1, d)),
            _const_spec((d, HEAD_LANES)), _const_spec((d, HEAD_LANES)),
        ],
        out_specs=[pl.BlockSpec((tm, d), row), pl.BlockSpec((tm, d), row),
                   pl.BlockSpec((tm, META_LANES), row),
                   pl.BlockSpec((1, HEAD_LANES), lambda i: (0, 0))],
        out_shape=[jax.ShapeDtypeStruct((n_tok, d), F32), jax.ShapeDtypeStruct((n_tok, d), F32),
                   jax.ShapeDtypeStruct((n_tok, META_LANES), F32),
                   jax.ShapeDtypeStruct((1, HEAD_LANES), F32)],
        scratch_shapes=[pltpu.VMEM((1, HEAD_LANES), F32)],
        compiler_params=_params("arbitrary"),
        name="oproj",
    )(o2d, x2d, wo, g, wr_hi, wr_lo)


def _moe_kernel(te_ref, nu_ref, h_ref, wg_ref, wu_ref, wd_ref, y_ref):
    @pl.when(pl.program_id(1) == 0)
    def _():
        y_ref[...] = jnp.zeros_like(y_ref)

    @pl.when(pl.program_id(0) < nu_ref[0])
    def _():
        h = h_ref[...].astype(BF16)
        for s0 in range(0, wg_ref.shape[2], FF_SUB):
            sub = slice(s0, s0 + FF_SUB)
            gate = jnp.dot(h, wg_ref[0, :, sub].astype(BF16), preferred_element_type=F32)
            up = jnp.dot(h, wu_ref[0, :, sub].astype(BF16), preferred_element_type=F32)
            act = (gate * jax.nn.sigmoid(gate) * up).astype(BF16)
            y_ref[...] += jnp.dot(act, wd_ref[0, sub, :].astype(BF16), preferred_element_type=F32)


def _moe(tile_expert, n_used, h_sorted, wg, wu, wd):
    n_slots, d = h_sorted.shape
    tm = TM_EXPERT
    fc = FF_CHUNK
    n_chunks = wg.shape[2] // fc
    def row_map(i, c, te, nu):
        return (jnp.minimum(i, nu[0] - 1), 0)
    def chunk(i, c, te, nu):
        return jnp.where(i < nu[0], c, n_chunks - 1)
    return pl.pallas_call(
        _moe_kernel,
        grid_spec=pltpu.PrefetchScalarGridSpec(
            num_scalar_prefetch=2,
            grid=(n_slots // tm, n_chunks),
            in_specs=[
                pl.BlockSpec((tm, d), row_map),
                pl.BlockSpec((1, d, fc), lambda i, c, te, nu: (te[i], 0, chunk(i, c, te, nu))),
                pl.BlockSpec((1, d, fc), lambda i, c, te, nu: (te[i], 0, chunk(i, c, te, nu))),
                pl.BlockSpec((1, fc, d), lambda i, c, te, nu: (te[i], chunk(i, c, te, nu), 0)),
            ],
            out_specs=pl.BlockSpec((tm, d), lambda i, c, te, nu: (i, 0)),
        ),
        out_shape=jax.ShapeDtypeStruct((n_slots, d), F32),
        compiler_params=_params("arbitrary", "arbitrary"),
        name="moe",
    )(tile_expert, n_used, h_sorted, wg, wu, wd)


def _final_kernel(x_ref, ya_ref, yb_ref, meta_ref, g_ref, o_ref):
    meta = meta_ref[...]
    x = x_ref[...] + meta[:, 4:5] * ya_ref[...] + meta[:, 5:6] * yb_ref[...]
    o_ref[...] = _rms(x) * g_ref[...]


def _final(x2d, ya, yb, meta, g):
    n_tok, d = x2d.shape
    tm = TM_TOKENS
    row = lambda i: (i, 0)
    big = pl.BlockSpec((tm, d), row)
    return pl.pallas_call(
        _final_kernel,
        grid=(n_tok // tm,),
        in_specs=[big, big, big, pl.BlockSpec((tm, META_LANES), row), _const_spec((1, d))],
        out_specs=big,
        out_shape=jax.ShapeDtypeStruct((n_tok, d), F32),
        compiler_params=_params("arbitrary"),
        name="final",
    )(x2d, ya, yb, meta, g)


def _gate_constants():
    wide = N_HEADS * HEAD_LANES
    pq = np.zeros((HEAD_LANES, wide), np.float32)
    pk = np.zeros((HEAD_LANES, wide), np.float32)
    cq = np.zeros((1, wide), np.float32)
    ck = np.zeros((1, wide), np.float32)
    cv = np.zeros((1, wide), np.float32)
    for h in range(N_HEADS):
        base = h * HEAD_LANES + (HEAD_DIM if h % 2 == 0 else 0)
        cv[0, base] = 1.0
        for piece in range(N_PIECES):
            r = piece * N_HEADS + h
            pq[r, base + piece] = 1.0
            ck[0, base + piece] = 1.0
            cq[0, base + N_PIECES + piece] = 1.0
            pk[r, base + N_PIECES + piece] = -1.0
    return (jnp.asarray(pq, BF16), jnp.asarray(pk, BF16), jnp.asarray(cq), jnp.asarray(ck), jnp.asarray(cv))


def kernel(x, norm_mix, norm_ffn, pool_w, pool_scale, norm_kv, w_kv, b_f, w_q, w_o, ffn_gate, ffn_up,
           ffn_down, w_router, exp_gate, exp_up, exp_down, norm_final):
    batch, seq, d = x.shape
    n_tok = batch * seq
    assert d == D_MODEL and seq % TM_TOKENS == 0 and seq % TQ_ATTN == 0 and TQ_ATTN % (2 * TK_ATTN) == 0
    row = lambda v: v.reshape(1, -1).astype(F32)
    x2d = x.reshape(n_tok, d)

    x2 = _layer0(x2d, seq, row(norm_mix[0]), row(norm_ffn[0]), pool_w[0].astype(BF16), row(pool_scale[0]),
                 ffn_gate[0].astype(BF16), ffn_up[0].astype(BF16), ffn_down[0].astype(BF16))

    wq = w_q[0].astype(BF16)
    wk = w_kv[:, :d].astype(BF16)
    wv = w_kv[:, d:2 * d].astype(BF16)
    wf = jnp.zeros((d, HEAD_LANES), F32).at[:, :N_PIECES * N_HEADS].set(
        jnp.tile(w_kv[:, 2 * d:], (1, N_PIECES))).astype(BF16)
    bf = jnp.zeros((1, HEAD_LANES), F32).at[0, :N_PIECES * N_HEADS].set(jnp.tile(b_f.astype(F32), N_PIECES))
    q, k, v = _qkv(x2, batch, seq, row(norm_mix[1]), row(norm_kv), wq, wk, wv, wf, bf, *_gate_constants())

    o = _attention(q, k, v)

    wr = jnp.zeros((d, HEAD_LANES), F32).at[:, :N_EXPERTS].set(w_router[0])
    wr_hi = wr.astype(BF16)
    wr_lo = (wr - wr_hi.astype(F32)).astype(BF16)
    x3, h3, meta, counts = _oproj(o.reshape(n_tok, d), x2, w_o[0].astype(BF16), row(norm_ffn[1]), wr_hi, wr_lo)

    tm = TM_EXPERT
    n_tiles = n_tok * TOP_K // tm + N_EXPERTS
    counts = counts[0, :N_EXPERTS].astype(jnp.int32)
    padded = (counts + tm - 1) // tm * tm
    group_end = jnp.cumsum(padded)
    group_start = group_end - padded
    expert = meta[:, 0:2].astype(jnp.int32)
    slot = group_start[expert] + meta[:, 2:4].astype(jnp.int32)
    n_used = (group_end[-1] // tm).astype(jnp.int32).reshape(1)
    tile_expert = jnp.sum(jnp.arange(n_tiles)[:, None] * tm >= group_end[None, :], axis=1)
    last_expert = jnp.max(jnp.where(counts > 0, jnp.arange(N_EXPERTS), 0))
    tile_expert = jnp.minimum(tile_expert, last_expert).astype(jnp.int32)
    src = jnp.zeros((n_tiles * tm,), jnp.int32).at[slot.reshape(-1)].set(
        jnp.repeat(jnp.arange(n_tok, dtype=jnp.int32), TOP_K))
    h_sorted = jnp.take(h3, src, axis=0, mode="clip")

    y = _moe(tile_expert, n_used, h_sorted, exp_gate[0], exp_up[0], exp_down[0])

    ya = jnp.take(y, slot[:, 0], axis=0, mode="clip")
    yb = jnp.take(y, slot[:, 1], axis=0, mode="clip")
    out = _final(x3, ya, yb, meta, row(norm_final))
    return out.reshape(batch, seq, d)
```

```python
import functools

import jax
import jax.numpy as jnp
import numpy as np
from jax import lax
from jax.experimental import pallas as pl
from jax.experimental.pallas import tpu as pltpu

F32 = jnp.float32
BF16 = jnp.bfloat16

D_MODEL = 1024
HEAD_DIM = 64
N_HEADS = D_MODEL // HEAD_DIM
HEAD_LANES = 128
POOL_WINDOWS = (2, 4, 8, 16)
POOL_GROUP = D_MODEL // len(POOL_WINDOWS)
POOL_HALO = 16
N_EXPERTS = 8
TOP_K = 2
EPS = 1e-5
NEG_INF = -1e30
LOG2E = 1.4426950408889634
STATS_ROWS = 8
SKIP_LOG2 = 152.0
NORM_SLACK = 1.02

N_PIECES = 3

TM_TOKENS = 512
TQ_ATTN = 1024
TK_ATTN = 512
TM_EXPERT = 1024
FF_CHUNK = 512
FF_SUB = 256

VMEM_LIMIT_BYTES = 56 * 1024 * 1024


def _params(*sem, flags=None):
    return pltpu.CompilerParams(dimension_semantics=sem, vmem_limit_bytes=VMEM_LIMIT_BYTES, flags=flags)


def _const_spec(shape):
    zeros = (0,) * len(shape)
    return pl.BlockSpec(shape, lambda *_: zeros, pipeline_mode=pl.Buffered(1))


def _rms(v):
    return v * lax.rsqrt(jnp.mean(v * v, axis=-1, keepdims=True) + EPS)


def _split3(v):
    hi = v.astype(BF16)
    r1 = v - hi.astype(F32)
    mid = r1.astype(BF16)
    lo = (r1 - mid.astype(F32)).astype(BF16)
    return hi, mid, lo


def _lower_tri(n, strict):
    r = lax.broadcasted_iota(jnp.int32, (n, n), 0)
    c = lax.broadcasted_iota(jnp.int32, (n, n), 1)
    return jnp.where((c < r) if strict else (c <= r), 1.0, 0.0).astype(BF16)


def _layer0_kernel(x_ref, xp_ref, gmix_ref, gffn_ref, pw_ref, ps_ref, wg_ref, wu_ref, wd_ref,
                   o_ref, *, tiles_per_seq):
    tm = x_ref.shape[0]
    t = pl.program_id(0) % tiles_per_seq
    x = x_ref[...]
    gmix = gmix_ref[...]
    h = _rms(x) * gmix
    hp = jnp.where(t == 0, 0.0, _rms(xp_ref[...]) * gmix)
    ext = jnp.concatenate([hp, h], axis=0)
    pos = t * tm + lax.broadcasted_iota(jnp.int32, (tm, 1), 0)
    mixed = []
    for g, w in enumerate(POOL_WINDOWS):
        lanes = slice(g * POOL_GROUP, (g + 1) * POOL_GROUP)
        win = ext[:, lanes]
        shift = 1
        while shift < w:
            win = win + pltpu.roll(win, shift, axis=0)
            shift *= 2
        cnt = jnp.minimum(pos + 1, w).astype(F32)
        pooled = win[POOL_HALO:, :] / cnt - h[:, lanes]
        mixed.append(jnp.dot(pooled.astype(BF16), pw_ref[g], preferred_element_type=F32))
    x1 = x + jnp.concatenate(mixed, axis=1) * ps_ref[...]
    h2 = (_rms(x1) * gffn_ref[...]).astype(BF16)
    gate = jnp.dot(h2, wg_ref[...], preferred_element_type=F32)
    up = jnp.dot(h2, wu_ref[...], preferred_element_type=F32)
    act = (gate * jax.nn.sigmoid(gate) * up).astype(BF16)
    o_ref[...] = x1 + jnp.dot(act, wd_ref[...], preferred_element_type=F32)


def _layer0(x2d, seq, gmix, gffn, pool_w, pool_scale, wg, wu, wd):
    n_tok, d = x2d.shape
    tm = TM_TOKENS
    tiles_per_seq = seq // tm
    halo_blocks = tm // POOL_HALO
    d_ff = wg.shape[1]
    return pl.pallas_call(
        functools.partial(_layer0_kernel, tiles_per_seq=tiles_per_seq),
        grid=(n_tok // tm,),
        in_specs=[
            pl.BlockSpec((tm, d), lambda i: (i, 0)),
            pl.BlockSpec((POOL_HALO, d), lambda i: (jnp.maximum(i * halo_blocks - 1, 0), 0)),
            _const_spec((1, d)), _const_spec((1, d)),
            _const_spec(pool_w.shape), _const_spec((1, d)),
            _const_spec((d, d_ff)), _const_spec((d, d_ff)), _const_spec((d_ff, d)),
        ],
        out_specs=pl.BlockSpec((tm, d), lambda i: (i, 0)),
        out_shape=jax.ShapeDtypeStruct((n_tok, d), F32),
        compiler_params=_params("arbitrary"),
        name="layer0",
    )(x2d, x2d, gmix, gffn, pool_w, pool_scale, wg, wu, wd)


def _qkv_kernel(x_ref, gq_ref, gkv_ref, wq_ref, wk_ref, wv_ref, wf_ref, bf_ref, pq_ref, pk_ref,
                cq_ref, ck_ref, cv_ref, hsum_ref, q_ref, k_ref, v_ref, stats_ref, carry_ref, *, tiles_per_seq):
    tm = x_ref.shape[0]

    @pl.when(pl.program_id(0) % tiles_per_seq == 0)
    def _():
        carry_ref[...] = jnp.zeros_like(carry_ref)

    xn = _rms(x_ref[...])
    hq = (xn * gq_ref[...]).astype(BF16)
    hkv = (xn * gkv_ref[...]).astype(BF16)

    z = jnp.dot(hkv, wf_ref[...], preferred_element_type=F32) + bf_ref[...]
    logf = jnp.minimum(z, 0.0) - jnp.log1p(jnp.exp(-jnp.abs(z)))
    tri = _lower_tri(tm, strict=False)
    fcum = carry_ref[...] + sum(jnp.dot(tri, p, preferred_element_type=F32) for p in _split3(logf))
    carry_ref[...] = fcum[tm - 1:tm, :]
    hi, mid, lo = _split3(fcum * LOG2E)
    lane = lax.broadcasted_iota(jnp.int32, fcum.shape, 1)
    pieces = jnp.where(lane < N_HEADS, hi, jnp.where(lane < 2 * N_HEADS, mid, lo))

    q = jnp.dot(hq, wq_ref[...], preferred_element_type=F32) * (HEAD_DIM ** -0.5 * LOG2E)
    k = jnp.dot(hkv, wk_ref[...], preferred_element_type=F32)
    v = jnp.dot(hkv, wv_ref[...], preferred_element_type=F32)
    gq = jnp.dot(pieces, pq_ref[...], preferred_element_type=F32) + cq_ref[...]
    gk = jnp.dot(pieces, pk_ref[...], preferred_element_type=F32) + ck_ref[...]
    low = lax.broadcasted_iota(jnp.int32, (tm, HEAD_LANES), 1) < HEAD_DIM
    for pair in range(N_HEADS // 2):
        data = slice(pair * HEAD_LANES, (pair + 1) * HEAD_LANES)
        for odd in range(2):
            h = 2 * pair + odd
            keep = jnp.logical_not(low) if odd else low
            extra = slice(h * HEAD_LANES, (h + 1) * HEAD_LANES)
            q_ref[0, h] = jnp.where(keep, q[:, data], gq[:, extra]).astype(BF16)
            k_ref[0, h] = jnp.where(keep, k[:, data], gk[:, extra]).astype(BF16)
            v_ref[0, h] = jnp.where(keep, v[:, data], cv_ref[:, extra]).astype(BF16)

    ksq = jnp.dot((k * k).astype(BF16), hsum_ref[...], preferred_element_type=F32)
    qsq = jnp.dot((q * q).astype(BF16), hsum_ref[...], preferred_element_type=F32)
    f2 = fcum * LOG2E
    rows = [jnp.max(ksq, axis=0, keepdims=True), jnp.max(qsq, axis=0, keepdims=True),
            f2[tm - 1:tm, :], f2[0:1, :]]
    stats_ref[0] = jnp.concatenate(rows + [jnp.zeros((STATS_ROWS - len(rows), HEAD_LANES), F32)], axis=0)


def _qkv(x2d, batch, seq, gq, gkv, wq, wk, wv, wf, bf, pq, pk, cq, ck, cv, hsum):
    n_tok, d = x2d.shape
    tm = TM_TOKENS
    tiles_per_seq = seq // tm
    wide = N_HEADS * HEAD_LANES
    head_spec = pl.BlockSpec((1, N_HEADS, tm, HEAD_LANES),
                             lambda i: (i // tiles_per_seq, 0, i % tiles_per_seq, 0))
    head_shape = jax.ShapeDtypeStruct((batch, N_HEADS, seq, HEAD_LANES), BF16)
    return pl.pallas_call(
        functools.partial(_qkv_kernel, tiles_per_seq=tiles_per_seq),
        grid=(n_tok // tm,),
        in_specs=[
            pl.BlockSpec((tm, d), lambda i: (i, 0)),
            _const_spec((1, d)), _const_spec((1, d)),
            _const_spec((d, d)), _const_spec((d, d)), _const_spec((d, d)),
            _const_spec((d, HEAD_LANES)), _const_spec((1, HEAD_LANES)),
            _const_spec((HEAD_LANES, wide)), _const_spec((HEAD_LANES, wide)),
            _const_spec((1, wide)), _const_spec((1, wide)), _const_spec((1, wide)),
            _const_spec((d, HEAD_LANES)),
        ],
        out_specs=[head_spec, head_spec, head_spec,
                   pl.BlockSpec((1, STATS_ROWS, HEAD_LANES), lambda i: (i, 0, 0))],
        out_shape=[head_shape, head_shape, head_shape,
                   jax.ShapeDtypeStruct((n_tok // tm, STATS_ROWS, HEAD_LANES), F32)],
        scratch_shapes=[pltpu.VMEM((1, HEAD_LANES), F32)],
        compiler_params=_params("arbitrary"),
        name="qkv",
    )(x2d, gq, gkv, wq, wk, wv, wf, bf, pq, pk, cq, ck, cv, hsum)


def _attn_kernel(q_ref, k_ref, v_ref, kstat_ref, qstat_ref, o_ref, p_buf, corr_buf, m_buf, acc_buf, *, tk):
    tq = q_ref.shape[2]
    qi = pl.program_id(2)
    chunks_per_q = tq // tk
    assert chunks_per_q == 2
    n_below = qi * chunks_per_q
    heads = range(2)
    tri = (lax.broadcasted_iota(jnp.int32, (tk, tk), 0) >= lax.broadcasted_iota(jnp.int32, (tk, tk), 1))
    lane = lax.broadcasted_iota(jnp.int32, (1, HEAD_LANES), 1)

    def scores(hh, j, rows=slice(None)):
        k = k_ref[0, hh, pl.ds(pl.multiple_of(j * tk, tk), tk), :]
        return lax.dot_general(q_ref[0, hh, rows, :], k, (((1,), (1,)), ((), ())),
                               preferred_element_type=F32)

    def mask_top(s):
        masked = jnp.where(tri, s[:tk], NEG_INF)
        return masked if s.shape[0] == tk else jnp.concatenate([masked, s[tk:]], axis=0)

    def softmax_chunk(s, m):
        blocks = [s[:, c:c + HEAD_LANES] for c in range(0, s.shape[1], HEAD_LANES)]
        m_new = jnp.maximum(m, jnp.max(functools.reduce(jnp.maximum, blocks), axis=-1, keepdims=True))
        p = jnp.concatenate([jnp.exp2(b - m_new) for b in blocks], axis=1)
        return m_new, p.astype(BF16), jnp.exp2(m - m_new)

    def value_matmul(hh, j, p):
        v = v_ref[0, hh, pl.ds(pl.multiple_of(j * tk, tk), tk), :]
        return jnp.dot(p, v, preferred_element_type=F32)

    def prepare(hh, j, slot):
        m, p, corr = softmax_chunk(scores(hh, j), m_buf[hh])
        m_buf[hh] = m
        p_buf[hh, slot] = p
        corr_buf[hh, slot] = corr

    def consume(hh, j, slot):
        acc_buf[hh] = acc_buf[hh] * corr_buf[hh, slot] + value_matmul(hh, j, p_buf[hh, slot])

    first_needed = []
    for hh in heads:
        m, p, _ = softmax_chunk(mask_top(scores(hh, n_below)), jnp.full((tq, HEAD_LANES), NEG_INF, F32))
        acc = value_matmul(hh, n_below, p)
        m_low, p, corr = softmax_chunk(mask_top(scores(hh, n_below + 1, rows=slice(tk, tq))), m[tk:])
        m = jnp.concatenate([m[:tk], m_low], axis=0)
        m_buf[hh] = m
        acc_buf[hh] = jnp.concatenate([acc[:tk], acc[tk:] * corr + value_matmul(hh, n_below + 1, p)], axis=0)

        kst = kstat_ref[0, hh]
        qst = qstat_ref[0, hh]
        mine = lane == qi
        qn = jnp.sum(jnp.where(mine, qst[0:1], 0.0), axis=1, keepdims=True)
        f0 = jnp.sum(jnp.where(mine, qst[1:2], 0.0), axis=1, keepdims=True)
        bound = qn * kst[0:1] * NORM_SLACK + (f0 - kst[1:2])
        floor = jnp.min(m, axis=0, keepdims=True) - SKIP_LOG2
        dead = (bound < floor) & (lane < n_below)
        first_needed.append(jnp.sum(dead.astype(jnp.int32)))

    start = jnp.minimum(first_needed[0], first_needed[1]) // 2 * 2
    n_pairs = (n_below - start) // 2

    @pl.when(n_pairs > 0)
    def _():
        for hh in heads:
            prepare(hh, n_below - 1, 0)

    def pair(jj, _):
        newest = n_below - 1 - 2 * jj
        for slot in range(2):
            for hh in heads:
                prepare(hh, jnp.maximum(newest - slot - 1, 0), 1 - slot)
                consume(hh, newest - slot, slot)
        return 0

    lax.fori_loop(0, n_pairs, pair, 0)
    outs = [acc_buf[hh] for hh in heads]
    even = outs[0] / outs[0][:, HEAD_DIM:HEAD_DIM + 1]
    odd = outs[1] / outs[1][:, 0:1]
    lane = lax.broadcasted_iota(jnp.int32, even.shape, 1)
    o_ref[0] = jnp.where(lane < HEAD_DIM, even, odd).astype(o_ref.dtype)


def _attention(q, k, v, kstat, qstat):
    batch, _, seq, _ = q.shape
    tq, tk = TQ_ATTN, TK_ATTN
    pair_kv = pl.BlockSpec((1, 2, seq, HEAD_LANES), lambda b, hp, i: (b, hp, 0, 0))
    pair_stat = pl.BlockSpec((1, 2, STATS_ROWS, HEAD_LANES), lambda b, hp, i: (b, hp, 0, 0))
    return pl.pallas_call(
        functools.partial(_attn_kernel, tk=tk),
        grid=(batch, N_HEADS // 2, seq // tq),
        in_specs=[pl.BlockSpec((1, 2, tq, HEAD_LANES), lambda b, hp, i: (b, hp, i, 0)), pair_kv, pair_kv,
                  pair_stat, pair_stat],
        out_specs=pl.BlockSpec((1, tq, HEAD_LANES), lambda b, hp, i: (b, i, hp)),
        out_shape=jax.ShapeDtypeStruct((batch, seq, D_MODEL), BF16),
        scratch_shapes=[pltpu.VMEM((2, 2, tq, tk), BF16),
                        pltpu.VMEM((2, 2, tq, HEAD_LANES), F32),
                        pltpu.VMEM((2, tq, HEAD_LANES), F32),
                        pltpu.VMEM((2, tq, HEAD_LANES), F32)],
        compiler_params=_params("arbitrary", "arbitrary", "arbitrary"),
        name="attn",
    )(q, k, v, kstat, qstat)


META_LANES = 8


def _oproj_kernel(o_ref, x_ref, wo_ref, g_ref, wrh_ref, wrl_ref,
                  x3_ref, h3_ref, meta_ref, cnt_ref, carry_ref):
    tm = x_ref.shape[0]

    @pl.when(pl.program_id(0) == 0)
    def _():
        carry_ref[...] = jnp.zeros_like(carry_ref)

    x3 = x_ref[...] + jnp.dot(o_ref[...], wo_ref[...], preferred_element_type=F32)
    x3_ref[...] = x3
    h3 = _rms(x3) * g_ref[...]
    h_hi = h3.astype(BF16)
    h_lo = (h3 - h_hi.astype(F32)).astype(BF16)
    h3_ref[...] = h3
    logits = (jnp.dot(h_hi, wrh_ref[...], preferred_element_type=F32)
              + jnp.dot(h_lo, wrh_ref[...], preferred_element_type=F32)
              + jnp.dot(h_hi, wrl_ref[...], preferred_element_type=F32))
    lane = lax.broadcasted_iota(jnp.int32, logits.shape, 1)
    lg = jnp.where(lane < N_EXPERTS, logits, -jnp.inf)
    m1 = jnp.max(lg, axis=-1, keepdims=True)
    i1 = jnp.min(jnp.where(lg == m1, lane, HEAD_LANES), axis=-1, keepdims=True)
    lg2 = jnp.where(lane == i1, -jnp.inf, lg)
    m2 = jnp.max(lg2, axis=-1, keepdims=True)
    i2 = jnp.min(jnp.where(lg2 == m2, lane, HEAD_LANES), axis=-1, keepdims=True)
    e2 = jnp.exp(m2 - m1)
    w1 = 1.0 / (1.0 + e2)
    w2 = e2 / (1.0 + e2)
    sel1 = lane == i1
    sel2 = lane == i2
    chosen = jnp.where(sel1 | sel2, 1.0, 0.0)
    before = carry_ref[...] + jnp.dot(_lower_tri(tm, strict=True), chosen.astype(BF16),
                                      preferred_element_type=F32)
    carry_ref[...] = before[tm - 1:tm, :] + chosen[tm - 1:tm, :]
    cnt_ref[...] = carry_ref[...]
    r1 = jnp.sum(jnp.where(sel1, before, 0.0), axis=-1, keepdims=True)
    r2 = jnp.sum(jnp.where(sel2, before, 0.0), axis=-1, keepdims=True)
    meta = jnp.zeros(logits.shape, F32)
    for idx, val in enumerate((i1.astype(F32), i2.astype(F32), r1, r2, w1, w2)):
        meta = jnp.where(lane == idx, val, meta)
    meta_ref[...] = meta[:, :META_LANES]


def _oproj(o2d, x2d, wo, g, wr_hi, wr_lo):
    n_tok, d = x2d.shape
    tm = TM_TOKENS
    row = lambda i: (i, 0)
    return pl.pallas_call(
        _oproj_kernel,
        grid=(n_tok // tm,),
        in_specs=[
            pl.BlockSpec((tm, d), row), pl.BlockSpec((tm, d), row),
            _const_spec((d, d)), _const_spec((1, d)),
            _const_spec((d, HEAD_LANES)), _const_spec((d, HEAD_LANES)),
        ],
        out_specs=[pl.BlockSpec((tm, d), row), pl.BlockSpec((tm, d), row),
                   pl.BlockSpec((tm, META_LANES), row),
                   pl.BlockSpec((1, HEAD_LANES), lambda i: (0, 0))],
        out_shape=[jax.ShapeDtypeStruct((n_tok, d), F32), jax.ShapeDtypeStruct((n_tok, d), F32),
                   jax.ShapeDtypeStruct((n_tok, META_LANES), F32),
                   jax.ShapeDtypeStruct((1, HEAD_LANES), F32)],
        scratch_shapes=[pltpu.VMEM((1, HEAD_LANES), F32)],
        compiler_params=_params("arbitrary"),
        name="oproj",
    )(o2d, x2d, wo, g, wr_hi, wr_lo)


def _moe_kernel(te_ref, nu_ref, h_ref, wg_ref, wu_ref, wd_ref, y_ref):
    @pl.when(pl.program_id(1) == 0)
    def _():
        y_ref[...] = jnp.zeros_like(y_ref)

    @pl.when(pl.program_id(0) < nu_ref[0])
    def _():
        h = h_ref[...].astype(BF16)
        for s0 in range(0, wg_ref.shape[2], FF_SUB):
            sub = slice(s0, s0 + FF_SUB)
            gate = jnp.dot(h, wg_ref[0, :, sub].astype(BF16), preferred_element_type=F32)
            up = jnp.dot(h, wu_ref[0, :, sub].astype(BF16), preferred_element_type=F32)
            act = (gate * jax.nn.sigmoid(gate) * up).astype(BF16)
            y_ref[...] += jnp.dot(act, wd_ref[0, sub, :].astype(BF16), preferred_element_type=F32)


def _moe(tile_expert, n_used, h_sorted, wg, wu, wd):
    n_slots, d = h_sorted.shape
    tm = TM_EXPERT
    fc = FF_CHUNK
    n_chunks = wg.shape[2] // fc
    def row_map(i, c, te, nu):
        return (jnp.minimum(i, nu[0] - 1), 0)
    def chunk(i, c, te, nu):
        return jnp.where(i < nu[0], c, n_chunks - 1)
    return pl.pallas_call(
        _moe_kernel,
        grid_spec=pltpu.PrefetchScalarGridSpec(
            num_scalar_prefetch=2,
            grid=(n_slots // tm, n_chunks),
            in_specs=[
                pl.BlockSpec((tm, d), row_map),
                pl.BlockSpec((1, d, fc), lambda i, c, te, nu: (te[i], 0, chunk(i, c, te, nu))),
                pl.BlockSpec((1, d, fc), lambda i, c, te, nu: (te[i], 0, chunk(i, c, te, nu))),
                pl.BlockSpec((1, fc, d), lambda i, c, te, nu: (te[i], chunk(i, c, te, nu), 0)),
            ],
            out_specs=pl.BlockSpec((tm, d), lambda i, c, te, nu: (i, 0)),
        ),
        out_shape=jax.ShapeDtypeStruct((n_slots, d), F32),
        compiler_params=_params("arbitrary", "arbitrary"),
        name="moe",
    )(tile_expert, n_used, h_sorted, wg, wu, wd)


def _final_kernel(x_ref, ya_ref, yb_ref, meta_ref, g_ref, o_ref):
    meta = meta_ref[...]
    x = x_ref[...] + meta[:, 4:5] * ya_ref[...] + meta[:, 5:6] * yb_ref[...]
    o_ref[...] = _rms(x) * g_ref[...]


def _final(x2d, ya, yb, meta, g):
    n_tok, d = x2d.shape
    tm = TM_TOKENS
    row = lambda i: (i, 0)
    big = pl.BlockSpec((tm, d), row)
    return pl.pallas_call(
        _final_kernel,
        grid=(n_tok // tm,),
        in_specs=[big, big, big, pl.BlockSpec((tm, META_LANES), row), _const_spec((1, d))],
        out_specs=big,
        out_shape=jax.ShapeDtypeStruct((n_tok, d), F32),
        compiler_params=_params("arbitrary"),
        name="final",
    )(x2d, ya, yb, meta, g)


def _gate_constants():
    wide = N_HEADS * HEAD_LANES
    pq = np.zeros((HEAD_LANES, wide), np.float32)
    pk = np.zeros((HEAD_LANES, wide), np.float32)
    cq = np.zeros((1, wide), np.float32)
    ck = np.zeros((1, wide), np.float32)
    cv = np.zeros((1, wide), np.float32)
    for h in range(N_HEADS):
        base = h * HEAD_LANES + (HEAD_DIM if h % 2 == 0 else 0)
        cv[0, base] = 1.0
        for piece in range(N_PIECES):
            r = piece * N_HEADS + h
            pq[r, base + piece] = 1.0
            ck[0, base + piece] = 1.0
            cq[0, base + N_PIECES + piece] = 1.0
            pk[r, base + N_PIECES + piece] = -1.0
    hsum = np.zeros((D_MODEL, HEAD_LANES), np.float32)
    hsum[np.arange(D_MODEL), np.arange(D_MODEL) // HEAD_DIM] = 1.0
    return (jnp.asarray(pq, BF16), jnp.asarray(pk, BF16), jnp.asarray(cq), jnp.asarray(ck), jnp.asarray(cv),
            jnp.asarray(hsum, BF16))


def _skip_tables(stats, batch, seq):
    tiles = seq // TM_TOKENS
    per_q = TQ_ATTN // TM_TOKENS
    st = stats.reshape(batch, tiles, STATS_ROWS, HEAD_LANES)[..., :N_HEADS]
    st = jnp.swapaxes(st, 1, 3)
    knorm = lax.cummax(jnp.sqrt(st[:, :, 0]), axis=2)
    qnorm = jnp.sqrt(jnp.max(st[:, :, 1].reshape(batch, N_HEADS, tiles // per_q, per_q), axis=-1))
    f_start = st[:, :, 3].reshape(batch, N_HEADS, tiles // per_q, per_q)[..., 0]

    def table(row0, row1):
        t = jnp.stack([row0, row1], axis=2)
        return jnp.pad(t, ((0, 0), (0, 0), (0, STATS_ROWS - 2), (0, HEAD_LANES - t.shape[-1])))

    return table(knorm, st[:, :, 2]), table(qnorm, f_start)


def kernel(x, norm_mix, norm_ffn, pool_w, pool_scale, norm_kv, w_kv, b_f, w_q, w_o, ffn_gate, ffn_up,
           ffn_down, w_router, exp_gate, exp_up, exp_down, norm_final):
    batch, seq, d = x.shape
    n_tok = batch * seq
    assert d == D_MODEL and seq % TQ_ATTN == 0 and TQ_ATTN == 2 * TK_ATTN and TK_ATTN == TM_TOKENS
    assert seq // TK_ATTN <= HEAD_LANES
    row = lambda v: v.reshape(1, -1).astype(F32)
    x2d = x.reshape(n_tok, d)

    x2 = _layer0(x2d, seq, row(norm_mix[0]), row(norm_ffn[0]), pool_w[0].astype(BF16), row(pool_scale[0]),
                 ffn_gate[0].astype(BF16), ffn_up[0].astype(BF16), ffn_down[0].astype(BF16))

    wq = w_q[0].astype(BF16)
    wk = w_kv[:, :d].astype(BF16)
    wv = w_kv[:, d:2 * d].astype(BF16)
    wf = jnp.zeros((d, HEAD_LANES), F32).at[:, :N_PIECES * N_HEADS].set(
        jnp.tile(w_kv[:, 2 * d:], (1, N_PIECES))).astype(BF16)
    bf = jnp.zeros((1, HEAD_LANES), F32).at[0, :N_PIECES * N_HEADS].set(jnp.tile(b_f.astype(F32), N_PIECES))
    q, k, v, stats = _qkv(x2, batch, seq, row(norm_mix[1]), row(norm_kv), wq, wk, wv, wf, bf,
                          *_gate_constants())

    o = _attention(q, k, v, *_skip_tables(stats, batch, seq))

    wr = jnp.zeros((d, HEAD_LANES), F32).at[:, :N_EXPERTS].set(w_router[0])
    wr_hi = wr.astype(BF16)
    wr_lo = (wr - wr_hi.astype(F32)).astype(BF16)
    x3, h3, meta, counts = _oproj(o.reshape(n_tok, d), x2, w_o[0].astype(BF16), row(norm_ffn[1]), wr_hi, wr_lo)

    tm = TM_EXPERT
    n_tiles = n_tok * TOP_K // tm + N_EXPERTS
    counts = counts[0, :N_EXPERTS].astype(jnp.int32)
    padded = (counts + tm - 1) // tm * tm
    group_end = jnp.cumsum(padded)
    group_start = group_end - padded
    expert = meta[:, 0:2].astype(jnp.int32)
    slot = group_start[expert] + meta[:, 2:4].astype(jnp.int32)
    n_used = (group_end[-1] // tm).astype(jnp.int32).reshape(1)
    tile_expert = jnp.sum(jnp.arange(n_tiles)[:, None] * tm >= group_end[None, :], axis=1)
    last_expert = jnp.max(jnp.where(counts > 0, jnp.arange(N_EXPERTS), 0))
    tile_expert = jnp.minimum(tile_expert, last_expert).astype(jnp.int32)
    src = (jnp.arange(n_tiles * tm, dtype=jnp.int32) % n_tok).at[slot.reshape(-1)].set(
        jnp.repeat(jnp.arange(n_tok, dtype=jnp.int32), TOP_K), unique_indices=True)
    h_sorted = jnp.take(h3, src, axis=0, mode="clip")

    y = _moe(tile_expert, n_used, h_sorted, exp_gate[0], exp_up[0], exp_down[0])

    ya = jnp.take(y, slot[:, 0], axis=0, mode="clip")
    yb = jnp.take(y, slot[:, 1], axis=0, mode="clip")
    out = _final(x3, ya, yb, meta, row(norm_final))
    return out.reshape(batch, seq, d)
```

```python
import functools

import jax
import jax.numpy as jnp
import numpy as np
from jax import lax
from jax.experimental import pallas as pl
from jax.experimental.pallas import tpu as pltpu

F32 = jnp.float32
BF16 = jnp.bfloat16

D_MODEL = 1024
HEAD_DIM = 64
N_HEADS = D_MODEL // HEAD_DIM
HEAD_LANES = 128
POOL_WINDOWS = (2, 4, 8, 16)
POOL_GROUP = D_MODEL // len(POOL_WINDOWS)
POOL_HALO = 16
N_EXPERTS = 8
TOP_K = 2
EPS = 1e-5
NEG_INF = -1e30
LOG2E = 1.4426950408889634
STATS_ROWS = 8
SKIP_LOG2 = 152.0
NORM_SLACK = 1.02

N_PIECES = 3

TM_TOKENS = 512
TQ_ATTN = 1024
TK_ATTN = 512
TM_EXPERT = 1024
FF_CHUNK = 512
FF_SUB = 256

VMEM_LIMIT_BYTES = 56 * 1024 * 1024


def _params(*sem, flags=None):
    return pltpu.CompilerParams(dimension_semantics=sem, vmem_limit_bytes=VMEM_LIMIT_BYTES, flags=flags)


def _const_spec(shape):
    zeros = (0,) * len(shape)
    return pl.BlockSpec(shape, lambda *_: zeros, pipeline_mode=pl.Buffered(1))


def _rms(v):
    return v * lax.rsqrt(jnp.mean(v * v, axis=-1, keepdims=True) + EPS)


def _split3(v):
    hi = v.astype(BF16)
    r1 = v - hi.astype(F32)
    mid = r1.astype(BF16)
    lo = (r1 - mid.astype(F32)).astype(BF16)
    return hi, mid, lo


def _lower_tri(n, strict):
    r = lax.broadcasted_iota(jnp.int32, (n, n), 0)
    c = lax.broadcasted_iota(jnp.int32, (n, n), 1)
    return jnp.where((c < r) if strict else (c <= r), 1.0, 0.0).astype(BF16)


def _layer0_kernel(x_ref, xp_ref, gmix_ref, gffn_ref, pw_ref, ps_ref, wg_ref, wu_ref, wd_ref,
                   o_ref, *, tiles_per_seq):
    tm = x_ref.shape[0]
    t = pl.program_id(0) % tiles_per_seq
    x = x_ref[...]
    gmix = gmix_ref[...]
    h = _rms(x) * gmix
    hp = jnp.where(t == 0, 0.0, _rms(xp_ref[...]) * gmix)
    ext = jnp.concatenate([hp, h], axis=0)
    pos = t * tm + lax.broadcasted_iota(jnp.int32, (tm, 1), 0)
    mixed = []
    for g, w in enumerate(POOL_WINDOWS):
        lanes = slice(g * POOL_GROUP, (g + 1) * POOL_GROUP)
        win = ext[:, lanes]
        shift = 1
        while shift < w:
            win = win + pltpu.roll(win, shift, axis=0)
            shift *= 2
        cnt = jnp.minimum(pos + 1, w).astype(F32)
        pooled = win[POOL_HALO:, :] / cnt - h[:, lanes]
        mixed.append(jnp.dot(pooled.astype(BF16), pw_ref[g], preferred_element_type=F32))
    x1 = x + jnp.concatenate(mixed, axis=1) * ps_ref[...]
    h2 = (_rms(x1) * gffn_ref[...]).astype(BF16)
    gate = jnp.dot(h2, wg_ref[...], preferred_element_type=F32)
    up = jnp.dot(h2, wu_ref[...], preferred_element_type=F32)
    act = (gate * jax.nn.sigmoid(gate) * up).astype(BF16)
    o_ref[...] = x1 + jnp.dot(act, wd_ref[...], preferred_element_type=F32)


def _layer0(x2d, seq, gmix, gffn, pool_w, pool_scale, wg, wu, wd):
    n_tok, d = x2d.shape
    tm = TM_TOKENS
    tiles_per_seq = seq // tm
    halo_blocks = tm // POOL_HALO
    d_ff = wg.shape[1]
    return pl.pallas_call(
        functools.partial(_layer0_kernel, tiles_per_seq=tiles_per_seq),
        grid=(n_tok // tm,),
        in_specs=[
            pl.BlockSpec((tm, d), lambda i: (i, 0)),
            pl.BlockSpec((POOL_HALO, d), lambda i: (jnp.maximum(i * halo_blocks - 1, 0), 0)),
            _const_spec((1, d)), _const_spec((1, d)),
            _const_spec(pool_w.shape), _const_spec((1, d)),
            _const_spec((d, d_ff)), _const_spec((d, d_ff)), _const_spec((d_ff, d)),
        ],
        out_specs=pl.BlockSpec((tm, d), lambda i: (i, 0)),
        out_shape=jax.ShapeDtypeStruct((n_tok, d), F32),
        compiler_params=_params("arbitrary"),
        name="layer0",
    )(x2d, x2d, gmix, gffn, pool_w, pool_scale, wg, wu, wd)


def _qkv_kernel(x_ref, gq_ref, gkv_ref, wq_ref, wk_ref, wv_ref, wf_ref, bf_ref, pq_ref, pk_ref,
                cq_ref, ck_ref, cv_ref, hsum_ref, q_ref, k_ref, v_ref, stats_ref, carry_ref, *, tiles_per_seq):
    tm = x_ref.shape[0]

    @pl.when(pl.program_id(0) % tiles_per_seq == 0)
    def _():
        carry_ref[...] = jnp.zeros_like(carry_ref)

    xn = _rms(x_ref[...])
    hq = (xn * gq_ref[...]).astype(BF16)
    hkv = (xn * gkv_ref[...]).astype(BF16)

    z = jnp.dot(hkv, wf_ref[...], preferred_element_type=F32) + bf_ref[...]
    logf = jnp.minimum(z, 0.0) - jnp.log1p(jnp.exp(-jnp.abs(z)))
    tri = _lower_tri(tm, strict=False)
    fcum = carry_ref[...] + sum(jnp.dot(tri, p, preferred_element_type=F32) for p in _split3(logf))
    carry_ref[...] = fcum[tm - 1:tm, :]
    hi, mid, lo = _split3(fcum * LOG2E)
    lane = lax.broadcasted_iota(jnp.int32, fcum.shape, 1)
    pieces = jnp.where(lane < N_HEADS, hi, jnp.where(lane < 2 * N_HEADS, mid, lo))

    q = jnp.dot(hq, wq_ref[...], preferred_element_type=F32) * (HEAD_DIM ** -0.5 * LOG2E)
    k = jnp.dot(hkv, wk_ref[...], preferred_element_type=F32)
    v = jnp.dot(hkv, wv_ref[...], preferred_element_type=F32)
    gq = jnp.dot(pieces, pq_ref[...], preferred_element_type=F32) + cq_ref[...]
    gk = jnp.dot(pieces, pk_ref[...], preferred_element_type=F32) + ck_ref[...]
    low = lax.broadcasted_iota(jnp.int32, (tm, HEAD_LANES), 1) < HEAD_DIM
    for pair in range(N_HEADS // 2):
        data = slice(pair * HEAD_LANES, (pair + 1) * HEAD_LANES)
        for odd in range(2):
            h = 2 * pair + odd
            keep = jnp.logical_not(low) if odd else low
            extra = slice(h * HEAD_LANES, (h + 1) * HEAD_LANES)
            q_ref[0, h] = jnp.where(keep, q[:, data], gq[:, extra]).astype(BF16)
            k_ref[0, h] = jnp.where(keep, k[:, data], gk[:, extra]).astype(BF16)
            v_ref[0, h] = jnp.where(keep, v[:, data], cv_ref[:, extra]).astype(BF16)

    ksq = jnp.dot((k * k).astype(BF16), hsum_ref[...], preferred_element_type=F32)
    qsq = jnp.dot((q * q).astype(BF16), hsum_ref[...], preferred_element_type=F32)
    f2 = fcum * LOG2E
    rows = [jnp.max(ksq, axis=0, keepdims=True), jnp.max(qsq, axis=0, keepdims=True),
            f2[tm - 1:tm, :], f2[0:1, :]]
    stats_ref[0] = jnp.concatenate(rows + [jnp.zeros((STATS_ROWS - len(rows), HEAD_LANES), F32)], axis=0)


def _qkv(x2d, batch, seq, gq, gkv, wq, wk, wv, wf, bf, pq, pk, cq, ck, cv, hsum):
    n_tok, d = x2d.shape
    tm = TM_TOKENS
    tiles_per_seq = seq // tm
    wide = N_HEADS * HEAD_LANES
    head_spec = pl.BlockSpec((1, N_HEADS, tm, HEAD_LANES),
                             lambda i: (i // tiles_per_seq, 0, i % tiles_per_seq, 0))
    head_shape = jax.ShapeDtypeStruct((batch, N_HEADS, seq, HEAD_LANES), BF16)
    return pl.pallas_call(
        functools.partial(_qkv_kernel, tiles_per_seq=tiles_per_seq),
        grid=(n_tok // tm,),
        in_specs=[
            pl.BlockSpec((tm, d), lambda i: (i, 0)),
            _const_spec((1, d)), _const_spec((1, d)),
            _const_spec((d, d)), _const_spec((d, d)), _const_spec((d, d)),
            _const_spec((d, HEAD_LANES)), _const_spec((1, HEAD_LANES)),
            _const_spec((HEAD_LANES, wide)), _const_spec((HEAD_LANES, wide)),
            _const_spec((1, wide)), _const_spec((1, wide)), _const_spec((1, wide)),
            _const_spec((d, HEAD_LANES)),
        ],
        out_specs=[head_spec, head_spec, head_spec,
                   pl.BlockSpec((1, STATS_ROWS, HEAD_LANES), lambda i: (i, 0, 0))],
        out_shape=[head_shape, head_shape, head_shape,
                   jax.ShapeDtypeStruct((n_tok // tm, STATS_ROWS, HEAD_LANES), F32)],
        scratch_shapes=[pltpu.VMEM((1, HEAD_LANES), F32)],
        compiler_params=_params("arbitrary"),
        name="qkv",
    )(x2d, gq, gkv, wq, wk, wv, wf, bf, pq, pk, cq, ck, cv, hsum)


def _attn_kernel(q_ref, k_ref, v_ref, kstat_ref, qstat_ref, o_ref, p_buf, corr_buf, m_buf, acc_buf, *, tk):
    tq = q_ref.shape[2]
    qi = pl.program_id(2)
    chunks_per_q = tq // tk
    assert chunks_per_q == 2
    n_below = qi * chunks_per_q
    heads = range(2)
    tri = (lax.broadcasted_iota(jnp.int32, (tk, tk), 0) >= lax.broadcasted_iota(jnp.int32, (tk, tk), 1))
    lane = lax.broadcasted_iota(jnp.int32, (1, HEAD_LANES), 1)

    def scores(hh, j, rows=slice(None)):
        k = k_ref[0, hh, pl.ds(pl.multiple_of(j * tk, tk), tk), :]
        return lax.dot_general(q_ref[0, hh, rows, :], k, (((1,), (1,)), ((), ())),
                               preferred_element_type=F32)

    def mask_top(s):
        masked = jnp.where(tri, s[:tk], NEG_INF)
        return masked if s.shape[0] == tk else jnp.concatenate([masked, s[tk:]], axis=0)

    def softmax_chunk(s, m):
        blocks = [s[:, c:c + HEAD_LANES] for c in range(0, s.shape[1], HEAD_LANES)]
        m_new = jnp.maximum(m, jnp.max(functools.reduce(jnp.maximum, blocks), axis=-1, keepdims=True))
        p = jnp.concatenate([jnp.exp2(b - m_new) for b in blocks], axis=1)
        return m_new, p.astype(BF16), jnp.exp2(m - m_new)

    def value_matmul(hh, j, p):
        v = v_ref[0, hh, pl.ds(pl.multiple_of(j * tk, tk), tk), :]
        return jnp.dot(p, v, preferred_element_type=F32)

    def prepare(hh, j, slot, valid=None):
        m_old = m_buf[hh]
        m, p, corr = softmax_chunk(scores(hh, j), m_old)
        m_buf[hh] = m if valid is None else jnp.where(valid, m, m_old)
        p_buf[hh, slot] = p
        corr_buf[hh, slot] = corr

    def consume(hh, j, slot):
        acc_buf[hh] = acc_buf[hh] * corr_buf[hh, slot] + value_matmul(hh, j, p_buf[hh, slot])

    first_needed = []
    for hh in heads:
        m, p, _ = softmax_chunk(mask_top(scores(hh, n_below)), jnp.full((tq, HEAD_LANES), NEG_INF, F32))
        acc = value_matmul(hh, n_below, p)
        m_low, p, corr = softmax_chunk(mask_top(scores(hh, n_below + 1, rows=slice(tk, tq))), m[tk:])
        m = jnp.concatenate([m[:tk], m_low], axis=0)
        m_buf[hh] = m
        acc_buf[hh] = jnp.concatenate([acc[:tk], acc[tk:] * corr + value_matmul(hh, n_below + 1, p)], axis=0)

        kst = kstat_ref[0, hh]
        qst = qstat_ref[0, hh]
        mine = lane == qi
        qn = jnp.sum(jnp.where(mine, qst[0:1], 0.0), axis=1, keepdims=True)
        f0 = jnp.sum(jnp.where(mine, qst[1:2], 0.0), axis=1, keepdims=True)
        bound = qn * kst[0:1] * NORM_SLACK + (f0 - kst[1:2])
        floor = jnp.min(m, axis=0, keepdims=True) - SKIP_LOG2
        dead = (bound < floor) & (lane < n_below)
        first_needed.append(jnp.sum(dead.astype(jnp.int32)))

    start = jnp.minimum(first_needed[0], first_needed[1]) // 2 * 2
    n_pairs = (n_below - start) // 2

    for hh in heads:
        prepare(hh, jnp.maximum(n_below - 1, 0), 0, valid=qi > 0)

    def pair(jj, _):
        newest = n_below - 1 - 2 * jj
        for slot in range(2):
            for hh in heads:
                prepare(hh, jnp.maximum(newest - slot - 1, 0), 1 - slot)
                consume(hh, newest - slot, slot)
        return 0

    lax.fori_loop(0, n_pairs, pair, 0)
    outs = [acc_buf[hh] for hh in heads]
    even = outs[0] / outs[0][:, HEAD_DIM:HEAD_DIM + 1]
    odd = outs[1] / outs[1][:, 0:1]
    lane = lax.broadcasted_iota(jnp.int32, even.shape, 1)
    o_ref[0] = jnp.where(lane < HEAD_DIM, even, odd).astype(o_ref.dtype)


def _attention(q, k, v, kstat, qstat):
    batch, _, seq, _ = q.shape
    tq, tk = TQ_ATTN, TK_ATTN
    pair_kv = pl.BlockSpec((1, 2, seq, HEAD_LANES), lambda b, hp, i: (b, hp, 0, 0))
    pair_stat = pl.BlockSpec((1, 2, STATS_ROWS, HEAD_LANES), lambda b, hp, i: (b, hp, 0, 0))
    return pl.pallas_call(
        functools.partial(_attn_kernel, tk=tk),
        grid=(batch, N_HEADS // 2, seq // tq),
        in_specs=[pl.BlockSpec((1, 2, tq, HEAD_LANES), lambda b, hp, i: (b, hp, i, 0)), pair_kv, pair_kv,
                  pair_stat, pair_stat],
        out_specs=pl.BlockSpec((1, tq, HEAD_LANES), lambda b, hp, i: (b, i, hp)),
        out_shape=jax.ShapeDtypeStruct((batch, seq, D_MODEL), BF16),
        scratch_shapes=[pltpu.VMEM((2, 2, tq, tk), BF16),
                        pltpu.VMEM((2, 2, tq, HEAD_LANES), F32),
                        pltpu.VMEM((2, tq, HEAD_LANES), F32),
                        pltpu.VMEM((2, tq, HEAD_LANES), F32)],
        compiler_params=_params("arbitrary", "arbitrary", "arbitrary"),
        name="attn",
    )(q, k, v, kstat, qstat)


META_LANES = 8


def _oproj_kernel(o_ref, x_ref, wo_ref, g_ref, wrh_ref, wrl_ref,
                  x3_ref, h3_ref, meta_ref, cnt_ref, carry_ref):
    tm = x_ref.shape[0]

    @pl.when(pl.program_id(0) == 0)
    def _():
        carry_ref[...] = jnp.zeros_like(carry_ref)

    x3 = x_ref[...] + jnp.dot(o_ref[...], wo_ref[...], preferred_element_type=F32)
    x3_ref[...] = x3
    h3 = _rms(x3) * g_ref[...]
    h_hi = h3.astype(BF16)
    h_lo = (h3 - h_hi.astype(F32)).astype(BF16)
    h3_ref[...] = h3
    nt = (((1,), (1,)), ((), ()))
    logits = (lax.dot_general(wrh_ref[...], h_hi, nt, preferred_element_type=F32)
              + lax.dot_general(wrh_ref[...], h_lo, nt, preferred_element_type=F32)
              + lax.dot_general(wrl_ref[...], h_hi, nt, preferred_element_type=F32))
    lg = logits[:N_EXPERTS]
    expert = lax.broadcasted_iota(jnp.int32, lg.shape, 0)
    m1 = jnp.max(lg, axis=0, keepdims=True)
    i1 = jnp.min(jnp.where(lg == m1, expert, N_EXPERTS), axis=0, keepdims=True)
    lg2 = jnp.where(expert == i1, -jnp.inf, lg)
    m2 = jnp.max(lg2, axis=0, keepdims=True)
    i2 = jnp.min(jnp.where(lg2 == m2, expert, N_EXPERTS), axis=0, keepdims=True)
    e2 = jnp.exp(m2 - m1)
    w1 = 1.0 / (1.0 + e2)
    w2 = e2 / (1.0 + e2)
    sel1 = expert == i1
    sel2 = expert == i2
    chosen = jnp.where(sel1 | sel2, 1.0, 0.0)
    chosen16 = jnp.concatenate([chosen, jnp.zeros_like(chosen)], axis=0).astype(BF16)
    row_t = lax.broadcasted_iota(jnp.int32, (tm, tm), 0)
    col_t = lax.broadcasted_iota(jnp.int32, (tm, tm), 1)
    earlier = jnp.where(row_t < col_t, 1.0, 0.0).astype(BF16)
    before = carry_ref[...][:, 0:1] + jnp.dot(chosen16, earlier, preferred_element_type=F32)[:N_EXPERTS]
    carry_ref[...] = carry_ref[...] + jnp.sum(chosen, axis=1, keepdims=True)
    cnt_ref[...] = carry_ref[...]
    r1 = jnp.sum(jnp.where(sel1, before, 0.0), axis=0, keepdims=True)
    r2 = jnp.sum(jnp.where(sel2, before, 0.0), axis=0, keepdims=True)
    meta_t = jnp.concatenate([i1.astype(F32), i2.astype(F32), r1, r2, w1, w2,
                              jnp.zeros((HEAD_LANES - 6, tm), F32)], axis=0)
    meta_ref[...] = meta_t.T[:, :META_LANES]


def _oproj(o2d, x2d, wo, g, wr_hi, wr_lo):
    n_tok, d = x2d.shape
    tm = TM_TOKENS
    row = lambda i: (i, 0)
    return pl.pallas_call(
        _oproj_kernel,
        grid=(n_tok // tm,),
        in_specs=[
            pl.BlockSpec((tm, d), row), pl.BlockSpec((tm, d), row),
            _const_spec((d, d)), _const_spec((1, d)),
            _const_spec(wr_hi.shape), _const_spec(wr_lo.shape),
        ],
        out_specs=[pl.BlockSpec((tm, d), row), pl.BlockSpec((tm, d), row),
                   pl.BlockSpec((tm, META_LANES), row),
                   pl.BlockSpec((N_EXPERTS, HEAD_LANES), lambda i: (0, 0))],
        out_shape=[jax.ShapeDtypeStruct((n_tok, d), F32), jax.ShapeDtypeStruct((n_tok, d), F32),
                   jax.ShapeDtypeStruct((n_tok, META_LANES), F32),
                   jax.ShapeDtypeStruct((N_EXPERTS, HEAD_LANES), F32)],
        scratch_shapes=[pltpu.VMEM((N_EXPERTS, HEAD_LANES), F32)],
        compiler_params=_params("arbitrary"),
        name="oproj",
    )(o2d, x2d, wo, g, wr_hi, wr_lo)


def _moe_kernel(te_ref, nu_ref, h_ref, wg_ref, wu_ref, wd_ref, y_ref):
    @pl.when(pl.program_id(1) == 0)
    def _():
        y_ref[...] = jnp.zeros_like(y_ref)

    @pl.when(pl.program_id(0) < nu_ref[0])
    def _():
        h = h_ref[...].astype(BF16)
        for s0 in range(0, wg_ref.shape[2], FF_SUB):
            sub = slice(s0, s0 + FF_SUB)
            gate = jnp.dot(h, wg_ref[0, :, sub].astype(BF16), preferred_element_type=F32)
            up = jnp.dot(h, wu_ref[0, :, sub].astype(BF16), preferred_element_type=F32)
            act = (gate * jax.nn.sigmoid(gate) * up).astype(BF16)
            y_ref[...] += jnp.dot(act, wd_ref[0, sub, :].astype(BF16), preferred_element_type=F32)


def _moe(tile_expert, n_used, h_sorted, wg, wu, wd):
    n_slots, d = h_sorted.shape
    tm = TM_EXPERT
    fc = FF_CHUNK
    n_chunks = wg.shape[2] // fc
    def row_map(i, c, te, nu):
        return (jnp.minimum(i, nu[0] - 1), 0)
    def chunk(i, c, te, nu):
        return jnp.where(i < nu[0], c, n_chunks - 1)
    return pl.pallas_call(
        _moe_kernel,
        grid_spec=pltpu.PrefetchScalarGridSpec(
            num_scalar_prefetch=2,
            grid=(n_slots // tm, n_chunks),
            in_specs=[
                pl.BlockSpec((tm, d), row_map),
                pl.BlockSpec((1, d, fc), lambda i, c, te, nu: (te[i], 0, chunk(i, c, te, nu))),
                pl.BlockSpec((1, d, fc), lambda i, c, te, nu: (te[i], 0, chunk(i, c, te, nu))),
                pl.BlockSpec((1, fc, d), lambda i, c, te, nu: (te[i], chunk(i, c, te, nu), 0)),
            ],
            out_specs=pl.BlockSpec((tm, d), lambda i, c, te, nu: (i, 0)),
        ),
        out_shape=jax.ShapeDtypeStruct((n_slots, d), F32),
        compiler_params=_params("arbitrary", "arbitrary"),
        name="moe",
    )(tile_expert, n_used, h_sorted, wg, wu, wd)


def _final_kernel(x_ref, ya_ref, yb_ref, meta_ref, g_ref, o_ref):
    meta = meta_ref[...]
    x = x_ref[...] + meta[:, 4:5] * ya_ref[...] + meta[:, 5:6] * yb_ref[...]
    o_ref[...] = _rms(x) * g_ref[...]


def _final(x2d, ya, yb, meta, g):
    n_tok, d = x2d.shape
    tm = TM_TOKENS
    row = lambda i: (i, 0)
    big = pl.BlockSpec((tm, d), row)
    return pl.pallas_call(
        _final_kernel,
        grid=(n_tok // tm,),
        in_specs=[big, big, big, pl.BlockSpec((tm, META_LANES), row), _const_spec((1, d))],
        out_specs=big,
        out_shape=jax.ShapeDtypeStruct((n_tok, d), F32),
        compiler_params=_params("arbitrary"),
        name="final",
    )(x2d, ya, yb, meta, g)


def _gate_constants():
    wide = N_HEADS * HEAD_LANES
    pq = np.zeros((HEAD_LANES, wide), np.float32)
    pk = np.zeros((HEAD_LANES, wide), np.float32)
    cq = np.zeros((1, wide), np.float32)
    ck = np.zeros((1, wide), np.float32)
    cv = np.zeros((1, wide), np.float32)
    for h in range(N_HEADS):
        base = h * HEAD_LANES + (HEAD_DIM if h % 2 == 0 else 0)
        cv[0, base] = 1.0
        for piece in range(N_PIECES):
            r = piece * N_HEADS + h
            pq[r, base + piece] = 1.0
            ck[0, base + piece] = 1.0
            cq[0, base + N_PIECES + piece] = 1.0
            pk[r, base + N_PIECES + piece] = -1.0
    hsum = np.zeros((D_MODEL, HEAD_LANES), np.float32)
    hsum[np.arange(D_MODEL), np.arange(D_MODEL) // HEAD_DIM] = 1.0
    return (jnp.asarray(pq, BF16), jnp.asarray(pk, BF16), jnp.asarray(cq), jnp.asarray(ck), jnp.asarray(cv),
            jnp.asarray(hsum, BF16))


def _skip_tables(stats, batch, seq):
    tiles = seq // TM_TOKENS
    per_q = TQ_ATTN // TM_TOKENS
    st = stats.reshape(batch, tiles, STATS_ROWS, HEAD_LANES)[..., :N_HEADS]
    st = jnp.swapaxes(st, 1, 3)
    knorm = lax.cummax(jnp.sqrt(st[:, :, 0]), axis=2)
    qnorm = jnp.sqrt(jnp.max(st[:, :, 1].reshape(batch, N_HEADS, tiles // per_q, per_q), axis=-1))
    f_start = st[:, :, 3].reshape(batch, N_HEADS, tiles // per_q, per_q)[..., 0]

    def table(row0, row1):
        t = jnp.stack([row0, row1], axis=2)
        return jnp.pad(t, ((0, 0), (0, 0), (0, STATS_ROWS - 2), (0, HEAD_LANES - t.shape[-1])))

    return table(knorm, st[:, :, 2]), table(qnorm, f_start)


def kernel(x, norm_mix, norm_ffn, pool_w, pool_scale, norm_kv, w_kv, b_f, w_q, w_o, ffn_gate, ffn_up,
           ffn_down, w_router, exp_gate, exp_up, exp_down, norm_final):
    batch, seq, d = x.shape
    n_tok = batch * seq
    assert d == D_MODEL and seq % TQ_ATTN == 0 and TQ_ATTN == 2 * TK_ATTN and TK_ATTN == TM_TOKENS
    assert seq // TK_ATTN <= HEAD_LANES
    row = lambda v: v.reshape(1, -1).astype(F32)
    x2d = x.reshape(n_tok, d)

    x2 = _layer0(x2d, seq, row(norm_mix[0]), row(norm_ffn[0]), pool_w[0].astype(BF16), row(pool_scale[0]),
                 ffn_gate[0].astype(BF16), ffn_up[0].astype(BF16), ffn_down[0].astype(BF16))

    wq = w_q[0].astype(BF16)
    wk = w_kv[:, :d].astype(BF16)
    wv = w_kv[:, d:2 * d].astype(BF16)
    wf = jnp.zeros((d, HEAD_LANES), F32).at[:, :N_PIECES * N_HEADS].set(
        jnp.tile(w_kv[:, 2 * d:], (1, N_PIECES))).astype(BF16)
    bf = jnp.zeros((1, HEAD_LANES), F32).at[0, :N_PIECES * N_HEADS].set(jnp.tile(b_f.astype(F32), N_PIECES))
    q, k, v, stats = _qkv(x2, batch, seq, row(norm_mix[1]), row(norm_kv), wq, wk, wv, wf, bf,
                          *_gate_constants())

    o = _attention(q, k, v, *_skip_tables(stats, batch, seq))

    wr = jnp.zeros((2 * N_EXPERTS, d), F32).at[:N_EXPERTS].set(w_router[0].T)
    wr_hi = wr.astype(BF16)
    wr_lo = (wr - wr_hi.astype(F32)).astype(BF16)
    x3, h3, meta, counts = _oproj(o.reshape(n_tok, d), x2, w_o[0].astype(BF16), row(norm_ffn[1]), wr_hi, wr_lo)

    tm = TM_EXPERT
    n_tiles = n_tok * TOP_K // tm + N_EXPERTS
    counts = counts[:, 0].astype(jnp.int32)
    padded = (counts + tm - 1) // tm * tm
    group_end = jnp.cumsum(padded)
    group_start = group_end - padded
    expert = meta[:, 0:2].astype(jnp.int32)
    slot = group_start[expert] + meta[:, 2:4].astype(jnp.int32)
    n_used = (group_end[-1] // tm).astype(jnp.int32).reshape(1)
    tile_expert = jnp.sum(jnp.arange(n_tiles)[:, None] * tm >= group_end[None, :], axis=1)
    last_expert = jnp.max(jnp.where(counts > 0, jnp.arange(N_EXPERTS), 0))
    tile_expert = jnp.minimum(tile_expert, last_expert).astype(jnp.int32)
    src = (jnp.arange(n_tiles * tm, dtype=jnp.int32) % n_tok).at[slot.reshape(-1)].set(
        jnp.repeat(jnp.arange(n_tok, dtype=jnp.int32), TOP_K), unique_indices=True)
    h_sorted = jnp.take(h3, src, axis=0, mode="clip")

    y = _moe(tile_expert, n_used, h_sorted, exp_gate[0], exp_up[0], exp_down[0])

    ya = jnp.take(y, slot[:, 0], axis=0, mode="clip")
    yb = jnp.take(y, slot[:, 1], axis=0, mode="clip")
    out = _final(x3, ya, yb, meta, row(norm_final))
    return out.reshape(batch, seq, d)
```

```python
import functools

import jax
import jax.numpy as jnp
import numpy as np
from jax import lax
from jax.experimental import pallas as pl
from jax.experimental.pallas import tpu as pltpu

F32 = jnp.float32
BF16 = jnp.bfloat16

D_MODEL = 1024
HEAD_DIM = 64
N_HEADS = D_MODEL // HEAD_DIM
HEAD_LANES = 128
POOL_WINDOWS = (2, 4, 8, 16)
POOL_GROUP = D_MODEL // len(POOL_WINDOWS)
POOL_HALO = 16
N_EXPERTS = 8
TOP_K = 2
EPS = 1e-5
NEG_INF = -1e30
LOG2E = 1.4426950408889634
STATS_ROWS = 8
SKIP_LOG2 = 152.0
NORM_SLACK = 1.02

N_PIECES = 3

TM_TOKENS = 512
TQ_ATTN = 1024
TK_ATTN = 512
TM_EXPERT = 1024
FF_CHUNK = 512
FF_SUB = 256

VMEM_LIMIT_BYTES = 56 * 1024 * 1024


def _params(*sem, flags=None):
    return pltpu.CompilerParams(dimension_semantics=sem, vmem_limit_bytes=VMEM_LIMIT_BYTES, flags=flags)


def _const_spec(shape):
    zeros = (0,) * len(shape)
    return pl.BlockSpec(shape, lambda *_: zeros, pipeline_mode=pl.Buffered(1))


def _rms(v):
    return v * lax.rsqrt(jnp.mean(v * v, axis=-1, keepdims=True) + EPS)


def _split3(v):
    hi = v.astype(BF16)
    r1 = v - hi.astype(F32)
    mid = r1.astype(BF16)
    lo = (r1 - mid.astype(F32)).astype(BF16)
    return hi, mid, lo


def _lower_tri(n, strict):
    r = lax.broadcasted_iota(jnp.int32, (n, n), 0)
    c = lax.broadcasted_iota(jnp.int32, (n, n), 1)
    return jnp.where((c < r) if strict else (c <= r), 1.0, 0.0).astype(BF16)


def _layer0_kernel(x_ref, xp_ref, gmix_ref, gffn_ref, pw_ref, ps_ref, wg_ref, wu_ref, wd_ref,
                   o_ref, *, tiles_per_seq):
    tm = x_ref.shape[0]
    t = pl.program_id(0) % tiles_per_seq
    x = x_ref[...]
    gmix = gmix_ref[...]
    h = _rms(x) * gmix
    hp = jnp.where(t == 0, 0.0, _rms(xp_ref[...]) * gmix)
    ext = jnp.concatenate([hp, h], axis=0)
    pos = t * tm + lax.broadcasted_iota(jnp.int32, (tm, 1), 0)
    mixed = []
    for g, w in enumerate(POOL_WINDOWS):
        lanes = slice(g * POOL_GROUP, (g + 1) * POOL_GROUP)
        win = ext[:, lanes]
        shift = 1
        while shift < w:
            win = win + pltpu.roll(win, shift, axis=0)
            shift *= 2
        cnt = jnp.minimum(pos + 1, w).astype(F32)
        pooled = win[POOL_HALO:, :] / cnt - h[:, lanes]
        mixed.append(jnp.dot(pooled.astype(BF16), pw_ref[g], preferred_element_type=F32))
    x1 = x + jnp.concatenate(mixed, axis=1) * ps_ref[...]
    h2 = (_rms(x1) * gffn_ref[...]).astype(BF16)
    gate = jnp.dot(h2, wg_ref[...], preferred_element_type=F32)
    up = jnp.dot(h2, wu_ref[...], preferred_element_type=F32)
    act = (gate * jax.nn.sigmoid(gate) * up).astype(BF16)
    o_ref[...] = x1 + jnp.dot(act, wd_ref[...], preferred_element_type=F32)


def _layer0(x2d, seq, gmix, gffn, pool_w, pool_scale, wg, wu, wd):
    n_tok, d = x2d.shape
    tm = TM_TOKENS
    tiles_per_seq = seq // tm
    halo_blocks = tm // POOL_HALO
    d_ff = wg.shape[1]
    return pl.pallas_call(
        functools.partial(_layer0_kernel, tiles_per_seq=tiles_per_seq),
        grid=(n_tok // tm,),
        in_specs=[
            pl.BlockSpec((tm, d), lambda i: (i, 0)),
            pl.BlockSpec((POOL_HALO, d), lambda i: (jnp.maximum(i * halo_blocks - 1, 0), 0)),
            _const_spec((1, d)), _const_spec((1, d)),
            _const_spec(pool_w.shape), _const_spec((1, d)),
            _const_spec((d, d_ff)), _const_spec((d, d_ff)), _const_spec((d_ff, d)),
        ],
        out_specs=pl.BlockSpec((tm, d), lambda i: (i, 0)),
        out_shape=jax.ShapeDtypeStruct((n_tok, d), F32),
        compiler_params=_params("arbitrary"),
        name="layer0",
    )(x2d, x2d, gmix, gffn, pool_w, pool_scale, wg, wu, wd)


def _qkv_kernel(x_ref, gq_ref, gkv_ref, wq_ref, wk_ref, wv_ref, wf_ref, bf_ref, pq_ref, pk_ref,
                cq_ref, ck_ref, cv_ref, hsum_ref, q_ref, k_ref, v_ref, stats_ref, carry_ref, *, tiles_per_seq):
    tm = x_ref.shape[0]

    @pl.when(pl.program_id(0) % tiles_per_seq == 0)
    def _():
        carry_ref[...] = jnp.zeros_like(carry_ref)

    xn = _rms(x_ref[...])
    hq = (xn * gq_ref[...]).astype(BF16)
    hkv = (xn * gkv_ref[...]).astype(BF16)

    z = jnp.dot(hkv, wf_ref[...], preferred_element_type=F32) + bf_ref[...]
    logf = jnp.minimum(z, 0.0) - jnp.log1p(jnp.exp(-jnp.abs(z)))
    tri = _lower_tri(tm, strict=False)
    fcum = carry_ref[...] + sum(jnp.dot(tri, p, preferred_element_type=F32) for p in _split3(logf))
    carry_ref[...] = fcum[tm - 1:tm, :]
    hi, mid, lo = _split3(fcum * LOG2E)
    lane = lax.broadcasted_iota(jnp.int32, fcum.shape, 1)
    pieces = jnp.where(lane < N_HEADS, hi, jnp.where(lane < 2 * N_HEADS, mid, lo))

    q = jnp.dot(hq, wq_ref[...], preferred_element_type=F32) * (HEAD_DIM ** -0.5 * LOG2E)
    k = jnp.dot(hkv, wk_ref[...], preferred_element_type=F32)
    v = jnp.dot(hkv, wv_ref[...], preferred_element_type=F32)
    gq = jnp.dot(pieces, pq_ref[...], preferred_element_type=F32) + cq_ref[...]
    gk = jnp.dot(pieces, pk_ref[...], preferred_element_type=F32) + ck_ref[...]
    low = lax.broadcasted_iota(jnp.int32, (tm, HEAD_LANES), 1) < HEAD_DIM
    for pair in range(N_HEADS // 2):
        data = slice(pair * HEAD_LANES, (pair + 1) * HEAD_LANES)
        for odd in range(2):
            h = 2 * pair + odd
            keep = jnp.logical_not(low) if odd else low
            q_ref[0, h] = jnp.where(keep, q[:, data], gq[:, data]).astype(BF16)
            k_ref[0, h] = jnp.where(keep, k[:, data], gk[:, data]).astype(BF16)
            v_ref[0, h] = jnp.where(keep, v[:, data], cv_ref[:, data]).astype(BF16)

    ksq = jnp.dot((k * k).astype(BF16), hsum_ref[...], preferred_element_type=F32)
    qsq = jnp.dot((q * q).astype(BF16), hsum_ref[...], preferred_element_type=F32)
    f2 = fcum * LOG2E
    rows = [jnp.max(ksq, axis=0, keepdims=True), jnp.max(qsq, axis=0, keepdims=True),
            f2[tm - 1:tm, :], f2[0:1, :]]
    stats_ref[0] = jnp.concatenate(rows + [jnp.zeros((STATS_ROWS - len(rows), HEAD_LANES), F32)], axis=0)


def _qkv(x2d, batch, seq, gq, gkv, wq, wk, wv, wf, bf, pq, pk, cq, ck, cv, hsum):
    n_tok, d = x2d.shape
    tm = TM_TOKENS
    tiles_per_seq = seq // tm
    wide = d
    head_spec = pl.BlockSpec((1, N_HEADS, tm, HEAD_LANES),
                             lambda i: (i // tiles_per_seq, 0, i % tiles_per_seq, 0))
    head_shape = jax.ShapeDtypeStruct((batch, N_HEADS, seq, HEAD_LANES), BF16)
    return pl.pallas_call(
        functools.partial(_qkv_kernel, tiles_per_seq=tiles_per_seq),
        grid=(n_tok // tm,),
        in_specs=[
            pl.BlockSpec((tm, d), lambda i: (i, 0)),
            _const_spec((1, d)), _const_spec((1, d)),
            _const_spec((d, d)), _const_spec((d, d)), _const_spec((d, d)),
            _const_spec((d, HEAD_LANES)), _const_spec((1, HEAD_LANES)),
            _const_spec((HEAD_LANES, wide)), _const_spec((HEAD_LANES, wide)),
            _const_spec((1, wide)), _const_spec((1, wide)), _const_spec((1, wide)),
            _const_spec((d, HEAD_LANES)),
        ],
        out_specs=[head_spec, head_spec, head_spec,
                   pl.BlockSpec((1, STATS_ROWS, HEAD_LANES), lambda i: (i, 0, 0))],
        out_shape=[head_shape, head_shape, head_shape,
                   jax.ShapeDtypeStruct((n_tok // tm, STATS_ROWS, HEAD_LANES), F32)],
        scratch_shapes=[pltpu.VMEM((1, HEAD_LANES), F32)],
        compiler_params=_params("arbitrary"),
        name="qkv",
    )(x2d, gq, gkv, wq, wk, wv, wf, bf, pq, pk, cq, ck, cv, hsum)


def _attn_kernel(q_ref, k_ref, v_ref, kstat_ref, qstat_ref, o_ref, p_buf, corr_buf, m_buf, acc_buf, *, tk):
    tq = q_ref.shape[2]
    qi = pl.program_id(2)
    chunks_per_q = tq // tk
    assert chunks_per_q == 2
    n_below = qi * chunks_per_q
    heads = range(2)
    tri = (lax.broadcasted_iota(jnp.int32, (tk, tk), 0) >= lax.broadcasted_iota(jnp.int32, (tk, tk), 1))
    lane = lax.broadcasted_iota(jnp.int32, (1, HEAD_LANES), 1)

    def scores(hh, j, rows=slice(None)):
        k = k_ref[0, hh, pl.ds(pl.multiple_of(j * tk, tk), tk), :]
        return lax.dot_general(q_ref[0, hh, rows, :], k, (((1,), (1,)), ((), ())),
                               preferred_element_type=F32)

    def mask_top(s):
        masked = jnp.where(tri, s[:tk], NEG_INF)
        return masked if s.shape[0] == tk else jnp.concatenate([masked, s[tk:]], axis=0)

    def softmax_chunk(s, m):
        blocks = [s[:, c:c + HEAD_LANES] for c in range(0, s.shape[1], HEAD_LANES)]
        m_new = jnp.maximum(m, jnp.max(functools.reduce(jnp.maximum, blocks), axis=-1, keepdims=True))
        p = jnp.concatenate([jnp.exp2(b - m_new) for b in blocks], axis=1)
        return m_new, p.astype(BF16), jnp.exp2(m - m_new)

    def value_matmul(hh, j, p):
        v = v_ref[0, hh, pl.ds(pl.multiple_of(j * tk, tk), tk), :]
        return jnp.dot(p, v, preferred_element_type=F32)

    def prepare(hh, j, slot, valid=None):
        m_old = m_buf[hh]
        m, p, corr = softmax_chunk(scores(hh, j), m_old)
        m_buf[hh] = m if valid is None else jnp.where(valid, m, m_old)
        p_buf[hh, slot] = p
        corr_buf[hh, slot] = corr

    def consume(hh, j, slot):
        acc_buf[hh] = acc_buf[hh] * corr_buf[hh, slot] + value_matmul(hh, j, p_buf[hh, slot])

    first_needed = []
    for hh in heads:
        m, p, _ = softmax_chunk(mask_top(scores(hh, n_below)), jnp.full((tq, HEAD_LANES), NEG_INF, F32))
        acc = value_matmul(hh, n_below, p)
        m_low, p, corr = softmax_chunk(mask_top(scores(hh, n_below + 1, rows=slice(tk, tq))), m[tk:])
        m = jnp.concatenate([m[:tk], m_low], axis=0)
        m_buf[hh] = m
        acc_buf[hh] = jnp.concatenate([acc[:tk], acc[tk:] * corr + value_matmul(hh, n_below + 1, p)], axis=0)

        kst = kstat_ref[0, hh]
        qst = qstat_ref[0, hh]
        mine = lane == qi
        qn = jnp.sum(jnp.where(mine, qst[0:1], 0.0), axis=1, keepdims=True)
        f0 = jnp.sum(jnp.where(mine, qst[1:2], 0.0), axis=1, keepdims=True)
        bound = qn * kst[0:1] * NORM_SLACK + (f0 - kst[1:2])
        floor = jnp.min(m, axis=0, keepdims=True) - SKIP_LOG2
        dead = (bound < floor) & (lane < n_below)
        first_needed.append(jnp.sum(dead.astype(jnp.int32)))

    starts = [f // 2 * 2 for f in first_needed]
    joint_start = jnp.maximum(starts[0], starts[1])

    for hh in heads:
        prepare(hh, jnp.maximum(n_below - 1, 0), 0, valid=qi > 0)

    def pair_body(which, top):
        def body(jj, _):
            newest = top - 1 - 2 * jj
            for slot in range(2):
                for hh in which:
                    prepare(hh, jnp.maximum(newest - slot - 1, 0), 1 - slot)
                    consume(hh, newest - slot, slot)
            return 0
        return body

    lax.fori_loop(0, (n_below - joint_start) // 2, pair_body(heads, n_below), 0)
    for hh in heads:
        lax.fori_loop(0, (joint_start - starts[hh]) // 2, pair_body((hh,), joint_start), 0)
    outs = [acc_buf[hh] for hh in heads]
    even = outs[0] / outs[0][:, HEAD_DIM:HEAD_DIM + 1]
    odd = outs[1] / outs[1][:, 0:1]
    lane = lax.broadcasted_iota(jnp.int32, even.shape, 1)
    o_ref[0] = jnp.where(lane < HEAD_DIM, even, odd).astype(o_ref.dtype)


def _attention(q, k, v, kstat, qstat):
    batch, _, seq, _ = q.shape
    tq, tk = TQ_ATTN, TK_ATTN
    pair_kv = pl.BlockSpec((1, 2, seq, HEAD_LANES), lambda b, hp, i: (b, hp, 0, 0))
    pair_stat = pl.BlockSpec((1, 2, STATS_ROWS, HEAD_LANES), lambda b, hp, i: (b, hp, 0, 0))
    return pl.pallas_call(
        functools.partial(_attn_kernel, tk=tk),
        grid=(batch, N_HEADS // 2, seq // tq),
        in_specs=[pl.BlockSpec((1, 2, tq, HEAD_LANES), lambda b, hp, i: (b, hp, i, 0)), pair_kv, pair_kv,
                  pair_stat, pair_stat],
        out_specs=pl.BlockSpec((1, tq, HEAD_LANES), lambda b, hp, i: (b, i, hp)),
        out_shape=jax.ShapeDtypeStruct((batch, seq, D_MODEL), BF16),
        scratch_shapes=[pltpu.VMEM((2, 2, tq, tk), BF16),
                        pltpu.VMEM((2, 2, tq, HEAD_LANES), F32),
                        pltpu.VMEM((2, tq, HEAD_LANES), F32),
                        pltpu.VMEM((2, tq, HEAD_LANES), F32)],
        compiler_params=_params("arbitrary", "arbitrary", "arbitrary"),
        name="attn",
    )(q, k, v, kstat, qstat)


META_LANES = 8


def _oproj_kernel(o_ref, x_ref, wo_ref, g_ref, wrh_ref, wrl_ref,
                  x3_ref, h3_ref, meta_ref, cnt_ref, carry_ref):
    tm = x_ref.shape[0]

    @pl.when(pl.program_id(0) == 0)
    def _():
        carry_ref[...] = jnp.zeros_like(carry_ref)

    x3 = x_ref[...] + jnp.dot(o_ref[...], wo_ref[...], preferred_element_type=F32)
    x3_ref[...] = x3
    h3 = _rms(x3) * g_ref[...]
    h_hi = h3.astype(BF16)
    h_lo = (h3 - h_hi.astype(F32)).astype(BF16)
    h3_ref[...] = h3
    nt = (((1,), (1,)), ((), ()))
    logits = (lax.dot_general(wrh_ref[...], h_hi, nt, preferred_element_type=F32)
              + lax.dot_general(wrh_ref[...], h_lo, nt, preferred_element_type=F32)
              + lax.dot_general(wrl_ref[...], h_hi, nt, preferred_element_type=F32))
    lg = logits[:N_EXPERTS]
    expert = lax.broadcasted_iota(jnp.int32, lg.shape, 0)
    m1 = jnp.max(lg, axis=0, keepdims=True)
    i1 = jnp.min(jnp.where(lg == m1, expert, N_EXPERTS), axis=0, keepdims=True)
    lg2 = jnp.where(expert == i1, -jnp.inf, lg)
    m2 = jnp.max(lg2, axis=0, keepdims=True)
    i2 = jnp.min(jnp.where(lg2 == m2, expert, N_EXPERTS), axis=0, keepdims=True)
    e2 = jnp.exp(m2 - m1)
    w1 = 1.0 / (1.0 + e2)
    w2 = e2 / (1.0 + e2)
    sel1 = expert == i1
    sel2 = expert == i2
    chosen = jnp.where(sel1 | sel2, 1.0, 0.0)
    chosen16 = jnp.concatenate([chosen, jnp.zeros_like(chosen)], axis=0).astype(BF16)
    row_t = lax.broadcasted_iota(jnp.int32, (tm, tm), 0)
    col_t = lax.broadcasted_iota(jnp.int32, (tm, tm), 1)
    earlier = jnp.where(row_t < col_t, 1.0, 0.0).astype(BF16)
    before = carry_ref[...][:, 0:1] + jnp.dot(chosen16, earlier, preferred_element_type=F32)[:N_EXPERTS]
    carry_ref[...] = carry_ref[...] + jnp.sum(chosen, axis=1, keepdims=True)
    cnt_ref[...] = carry_ref[...]
    r1 = jnp.sum(jnp.where(sel1, before, 0.0), axis=0, keepdims=True)
    r2 = jnp.sum(jnp.where(sel2, before, 0.0), axis=0, keepdims=True)
    meta_t = jnp.concatenate([i1.astype(F32), i2.astype(F32), r1, r2, w1, w2,
                              jnp.zeros((HEAD_LANES - 6, tm), F32)], axis=0)
    meta_ref[...] = meta_t.T[:, :META_LANES]


def _oproj(o2d, x2d, wo, g, wr_hi, wr_lo):
    n_tok, d = x2d.shape
    tm = TM_TOKENS
    row = lambda i: (i, 0)
    return pl.pallas_call(
        _oproj_kernel,
        grid=(n_tok // tm,),
        in_specs=[
            pl.BlockSpec((tm, d), row), pl.BlockSpec((tm, d), row),
            _const_spec((d, d)), _const_spec((1, d)),
            _const_spec(wr_hi.shape), _const_spec(wr_lo.shape),
        ],
        out_specs=[pl.BlockSpec((tm, d), row), pl.BlockSpec((tm, d), row),
                   pl.BlockSpec((tm, META_LANES), row),
                   pl.BlockSpec((N_EXPERTS, HEAD_LANES), lambda i: (0, 0))],
        out_shape=[jax.ShapeDtypeStruct((n_tok, d), F32), jax.ShapeDtypeStruct((n_tok, d), F32),
                   jax.ShapeDtypeStruct((n_tok, META_LANES), F32),
                   jax.ShapeDtypeStruct((N_EXPERTS, HEAD_LANES), F32)],
        scratch_shapes=[pltpu.VMEM((N_EXPERTS, HEAD_LANES), F32)],
        compiler_params=_params("arbitrary"),
        name="oproj",
    )(o2d, x2d, wo, g, wr_hi, wr_lo)


def _moe_kernel(te_ref, nu_ref, h_ref, wg_ref, wu_ref, wd_ref, y_ref):
    @pl.when(pl.program_id(1) == 0)
    def _():
        y_ref[...] = jnp.zeros_like(y_ref)

    @pl.when(pl.program_id(0) < nu_ref[0])
    def _():
        h = h_ref[...].astype(BF16)
        for s0 in range(0, wg_ref.shape[2], FF_SUB):
            sub = slice(s0, s0 + FF_SUB)
            gate = jnp.dot(h, wg_ref[0, :, sub].astype(BF16), preferred_element_type=F32)
            up = jnp.dot(h, wu_ref[0, :, sub].astype(BF16), preferred_element_type=F32)
            act = (gate * jax.nn.sigmoid(gate) * up).astype(BF16)
            y_ref[...] += jnp.dot(act, wd_ref[0, sub, :].astype(BF16), preferred_element_type=F32)


def _moe(tile_expert, n_used, h_sorted, wg, wu, wd):
    n_slots, d = h_sorted.shape
    tm = TM_EXPERT
    fc = FF_CHUNK
    n_chunks = wg.shape[2] // fc
    def row_map(i, c, te, nu):
        return (jnp.minimum(i, nu[0] - 1), 0)
    def chunk(i, c, te, nu):
        return jnp.where(i < nu[0], c, n_chunks - 1)
    return pl.pallas_call(
        _moe_kernel,
        grid_spec=pltpu.PrefetchScalarGridSpec(
            num_scalar_prefetch=2,
            grid=(n_slots // tm, n_chunks),
            in_specs=[
                pl.BlockSpec((tm, d), row_map),
                pl.BlockSpec((1, d, fc), lambda i, c, te, nu: (te[i], 0, chunk(i, c, te, nu))),
                pl.BlockSpec((1, d, fc), lambda i, c, te, nu: (te[i], 0, chunk(i, c, te, nu))),
                pl.BlockSpec((1, fc, d), lambda i, c, te, nu: (te[i], chunk(i, c, te, nu), 0)),
            ],
            out_specs=pl.BlockSpec((tm, d), lambda i, c, te, nu: (i, 0)),
        ),
        out_shape=jax.ShapeDtypeStruct((n_slots, d), F32),
        compiler_params=_params("arbitrary", "arbitrary"),
        name="moe",
    )(tile_expert, n_used, h_sorted, wg, wu, wd)


def _final_kernel(x_ref, ya_ref, yb_ref, meta_ref, g_ref, o_ref):
    meta = meta_ref[...]
    x = x_ref[...] + meta[:, 4:5] * ya_ref[...] + meta[:, 5:6] * yb_ref[...]
    o_ref[...] = _rms(x) * g_ref[...]


def _final(x2d, ya, yb, meta, g):
    n_tok, d = x2d.shape
    tm = TM_TOKENS
    row = lambda i: (i, 0)
    big = pl.BlockSpec((tm, d), row)
    return pl.pallas_call(
        _final_kernel,
        grid=(n_tok // tm,),
        in_specs=[big, big, big, pl.BlockSpec((tm, META_LANES), row), _const_spec((1, d))],
        out_specs=big,
        out_shape=jax.ShapeDtypeStruct((n_tok, d), F32),
        compiler_params=_params("arbitrary"),
        name="final",
    )(x2d, ya, yb, meta, g)


def _gate_constants():
    wide = D_MODEL
    pq = np.zeros((HEAD_LANES, wide), np.float32)
    pk = np.zeros((HEAD_LANES, wide), np.float32)
    cq = np.zeros((1, wide), np.float32)
    ck = np.zeros((1, wide), np.float32)
    cv = np.zeros((1, wide), np.float32)
    for h in range(N_HEADS):
        base = (h // 2) * HEAD_LANES + (HEAD_DIM if h % 2 == 0 else 0)
        cv[0, base] = 1.0
        for piece in range(N_PIECES):
            r = piece * N_HEADS + h
            pq[r, base + piece] = 1.0
            ck[0, base + piece] = 1.0
            cq[0, base + N_PIECES + piece] = 1.0
            pk[r, base + N_PIECES + piece] = -1.0
    hsum = np.zeros((D_MODEL, HEAD_LANES), np.float32)
    hsum[np.arange(D_MODEL), np.arange(D_MODEL) // HEAD_DIM] = 1.0
    return (jnp.asarray(pq, BF16), jnp.asarray(pk, BF16), jnp.asarray(cq), jnp.asarray(ck), jnp.asarray(cv),
            jnp.asarray(hsum, BF16))


def _skip_tables(stats, batch, seq):
    tiles = seq // TM_TOKENS
    per_q = TQ_ATTN // TM_TOKENS
    st = stats.reshape(batch, tiles, STATS_ROWS, HEAD_LANES)[..., :N_HEADS]
    st = jnp.swapaxes(st, 1, 3)
    knorm = lax.cummax(jnp.sqrt(st[:, :, 0]), axis=2)
    qnorm = jnp.sqrt(jnp.max(st[:, :, 1].reshape(batch, N_HEADS, tiles // per_q, per_q), axis=-1))
    f_start = st[:, :, 3].reshape(batch, N_HEADS, tiles // per_q, per_q)[..., 0]

    def table(row0, row1):
        t = jnp.stack([row0, row1], axis=2)
        return jnp.pad(t, ((0, 0), (0, 0), (0, STATS_ROWS - 2), (0, HEAD_LANES - t.shape[-1])))

    return table(knorm, st[:, :, 2]), table(qnorm, f_start)


def kernel(x, norm_mix, norm_ffn, pool_w, pool_scale, norm_kv, w_kv, b_f, w_q, w_o, ffn_gate, ffn_up,
           ffn_down, w_router, exp_gate, exp_up, exp_down, norm_final):
    batch, seq, d = x.shape
    n_tok = batch * seq
    assert d == D_MODEL and seq % TQ_ATTN == 0 and TQ_ATTN == 2 * TK_ATTN and TK_ATTN == TM_TOKENS
    assert seq // TK_ATTN <= HEAD_LANES
    row = lambda v: v.reshape(1, -1).astype(F32)
    x2d = x.reshape(n_tok, d)

    x2 = _layer0(x2d, seq, row(norm_mix[0]), row(norm_ffn[0]), pool_w[0].astype(BF16), row(pool_scale[0]),
                 ffn_gate[0].astype(BF16), ffn_up[0].astype(BF16), ffn_down[0].astype(BF16))

    wq = w_q[0].astype(BF16)
    wk = w_kv[:, :d].astype(BF16)
    wv = w_kv[:, d:2 * d].astype(BF16)
    wf = jnp.zeros((d, HEAD_LANES), F32).at[:, :N_PIECES * N_HEADS].set(
        jnp.tile(w_kv[:, 2 * d:], (1, N_PIECES))).astype(BF16)
    bf = jnp.zeros((1, HEAD_LANES), F32).at[0, :N_PIECES * N_HEADS].set(jnp.tile(b_f.astype(F32), N_PIECES))
    q, k, v, stats = _qkv(x2, batch, seq, row(norm_mix[1]), row(norm_kv), wq, wk, wv, wf, bf,
                          *_gate_constants())

    o = _attention(q, k, v, *_skip_tables(stats, batch, seq))

    wr = jnp.zeros((2 * N_EXPERTS, d), F32).at[:N_EXPERTS].set(w_router[0].T)
    wr_hi = wr.astype(BF16)
    wr_lo = (wr - wr_hi.astype(F32)).astype(BF16)
    x3, h3, meta, counts = _oproj(o.reshape(n_tok, d), x2, w_o[0].astype(BF16), row(norm_ffn[1]), wr_hi, wr_lo)

    tm = TM_EXPERT
    n_tiles = n_tok * TOP_K // tm + N_EXPERTS
    counts = counts[:, 0].astype(jnp.int32)
    padded = (counts + tm - 1) // tm * tm
    group_end = jnp.cumsum(padded)
    group_start = group_end - padded
    expert = meta[:, 0:2].astype(jnp.int32)
    slot = group_start[expert] + meta[:, 2:4].astype(jnp.int32)
    n_used = (group_end[-1] // tm).astype(jnp.int32).reshape(1)
    tile_expert = jnp.sum(jnp.arange(n_tiles)[:, None] * tm >= group_end[None, :], axis=1)
    last_expert = jnp.max(jnp.where(counts > 0, jnp.arange(N_EXPERTS), 0))
    tile_expert = jnp.minimum(tile_expert, last_expert).astype(jnp.int32)
    src = (jnp.arange(n_tiles * tm, dtype=jnp.int32) % n_tok).at[slot.reshape(-1)].set(
        jnp.repeat(jnp.arange(n_tok, dtype=jnp.int32), TOP_K), unique_indices=True)
    h_sorted = jnp.take(h3, src, axis=0, mode="clip")

    y = _moe(tile_expert, n_used, h_sorted, exp_gate[0], exp_up[0], exp_down[0])

    ya = jnp.take(y, slot[:, 0], axis=0, mode="clip")
    yb = jnp.take(y, slot[:, 1], axis=0, mode="clip")
    out = _final(x3, ya, yb, meta, row(norm_final))
    return out.reshape(batch, seq, d)
```

```python
import functools

import jax
import jax.numpy as jnp
import numpy as np
from jax import lax
from jax.experimental import pallas as pl
from jax.experimental.pallas import tpu as pltpu

F32 = jnp.float32
BF16 = jnp.bfloat16

D_MODEL = 1024
HEAD_DIM = 64
N_HEADS = D_MODEL // HEAD_DIM
HEAD_LANES = 128
POOL_WINDOWS = (2, 4, 8, 16)
POOL_GROUP = D_MODEL // len(POOL_WINDOWS)
POOL_HALO = 16
N_EXPERTS = 8
TOP_K = 2
EPS = 1e-5
NEG_INF = -1e30
LOG2E = 1.4426950408889634
STATS_ROWS = 8
SKIP_LOG2 = 152.0
NORM_SLACK = 1.02

N_PIECES = 3

TM_TOKENS = 512
TQ_ATTN = 1024
TK_ATTN = 512
TM_EXPERT = 1024
FF_CHUNK = 512
FF_SUB = 256

VMEM_LIMIT_BYTES = 56 * 1024 * 1024


def _params(*sem, flags=None):
    return pltpu.CompilerParams(dimension_semantics=sem, vmem_limit_bytes=VMEM_LIMIT_BYTES, flags=flags)


def _const_spec(shape):
    zeros = (0,) * len(shape)
    return pl.BlockSpec(shape, lambda *_: zeros, pipeline_mode=pl.Buffered(1))


def _rms(v):
    return v * lax.rsqrt(jnp.mean(v * v, axis=-1, keepdims=True) + EPS)


def _split3(v):
    hi = v.astype(BF16)
    r1 = v - hi.astype(F32)
    mid = r1.astype(BF16)
    lo = (r1 - mid.astype(F32)).astype(BF16)
    return hi, mid, lo


def _lower_tri(n, strict):
    r = lax.broadcasted_iota(jnp.int32, (n, n), 0)
    c = lax.broadcasted_iota(jnp.int32, (n, n), 1)
    return jnp.where((c < r) if strict else (c <= r), 1.0, 0.0).astype(BF16)


def _layer0_kernel(x_ref, xp_ref, gmix_ref, gffn_ref, pw_ref, ps_ref, wg_ref, wu_ref, wd_ref,
                   o_ref, *, tiles_per_seq):
    tm = x_ref.shape[0]
    t = pl.program_id(0) % tiles_per_seq
    x = x_ref[...]
    gmix = gmix_ref[...]
    h = _rms(x) * gmix
    hp = jnp.where(t == 0, 0.0, _rms(xp_ref[...]) * gmix)
    ext = jnp.concatenate([hp, h], axis=0)
    pos = t * tm + lax.broadcasted_iota(jnp.int32, (tm, 1), 0)
    mixed = []
    for g, w in enumerate(POOL_WINDOWS):
        lanes = slice(g * POOL_GROUP, (g + 1) * POOL_GROUP)
        win = ext[:, lanes]
        shift = 1
        while shift < w:
            win = win + pltpu.roll(win, shift, axis=0)
            shift *= 2
        cnt = jnp.minimum(pos + 1, w).astype(F32)
        pooled = win[POOL_HALO:, :] / cnt - h[:, lanes]
        mixed.append(jnp.dot(pooled.astype(BF16), pw_ref[g], preferred_element_type=F32))
    x1 = x + jnp.concatenate(mixed, axis=1) * ps_ref[...]
    h2 = (_rms(x1) * gffn_ref[...]).astype(BF16)
    gate = jnp.dot(h2, wg_ref[...], preferred_element_type=F32)
    up = jnp.dot(h2, wu_ref[...], preferred_element_type=F32)
    act = (gate * jax.nn.sigmoid(gate) * up).astype(BF16)
    o_ref[...] = x1 + jnp.dot(act, wd_ref[...], preferred_element_type=F32)


def _layer0(x2d, seq, gmix, gffn, pool_w, pool_scale, wg, wu, wd):
    n_tok, d = x2d.shape
    tm = TM_TOKENS
    tiles_per_seq = seq // tm
    halo_blocks = tm // POOL_HALO
    d_ff = wg.shape[1]
    return pl.pallas_call(
        functools.partial(_layer0_kernel, tiles_per_seq=tiles_per_seq),
        grid=(n_tok // tm,),
        in_specs=[
            pl.BlockSpec((tm, d), lambda i: (i, 0)),
            pl.BlockSpec((POOL_HALO, d), lambda i: (jnp.maximum(i * halo_blocks - 1, 0), 0)),
            _const_spec((1, d)), _const_spec((1, d)),
            _const_spec(pool_w.shape), _const_spec((1, d)),
            _const_spec((d, d_ff)), _const_spec((d, d_ff)), _const_spec((d_ff, d)),
        ],
        out_specs=pl.BlockSpec((tm, d), lambda i: (i, 0)),
        out_shape=jax.ShapeDtypeStruct((n_tok, d), F32),
        compiler_params=_params("arbitrary"),
        name="layer0",
    )(x2d, x2d, gmix, gffn, pool_w, pool_scale, wg, wu, wd)


def _qkv_kernel(x_ref, gq_ref, gkv_ref, wq_ref, wk_ref, wv_ref, wf_ref, bf_ref, pq_ref, pk_ref,
                cq_ref, ck_ref, cv_ref, hsum_ref, q_ref, k_ref, v_ref, stats_ref, carry_ref, *, tiles_per_seq):
    tm = x_ref.shape[0]

    @pl.when(pl.program_id(0) % tiles_per_seq == 0)
    def _():
        carry_ref[...] = jnp.zeros_like(carry_ref)

    xn = _rms(x_ref[...])
    hq = (xn * gq_ref[...]).astype(BF16)
    hkv = (xn * gkv_ref[...]).astype(BF16)

    z = jnp.dot(hkv, wf_ref[...], preferred_element_type=F32) + bf_ref[...]
    logf = jnp.minimum(z, 0.0) - jnp.log1p(jnp.exp(-jnp.abs(z)))
    tri = _lower_tri(tm, strict=False)
    fcum = carry_ref[...] + sum(jnp.dot(tri, p, preferred_element_type=F32) for p in _split3(logf))
    carry_ref[...] = fcum[tm - 1:tm, :]
    hi, mid, lo = _split3(fcum * LOG2E)
    lane = lax.broadcasted_iota(jnp.int32, fcum.shape, 1)
    pieces = jnp.where(lane < N_HEADS, hi, jnp.where(lane < 2 * N_HEADS, mid, lo))

    q = jnp.dot(hq, wq_ref[...], preferred_element_type=F32) * (HEAD_DIM ** -0.5 * LOG2E)
    k = jnp.dot(hkv, wk_ref[...], preferred_element_type=F32)
    v = jnp.dot(hkv, wv_ref[...], preferred_element_type=F32)
    gq = jnp.dot(pieces, pq_ref[...], preferred_element_type=F32) + cq_ref[...]
    gk = jnp.dot(pieces, pk_ref[...], preferred_element_type=F32) + ck_ref[...]
    low = lax.broadcasted_iota(jnp.int32, (tm, HEAD_LANES), 1) < HEAD_DIM
    for pair in range(N_HEADS // 2):
        data = slice(pair * HEAD_LANES, (pair + 1) * HEAD_LANES)
        for odd in range(2):
            h = 2 * pair + odd
            keep = jnp.logical_not(low) if odd else low
            q_ref[0, h] = jnp.where(keep, q[:, data], gq[:, data]).astype(BF16)
            k_ref[0, h] = jnp.where(keep, k[:, data], gk[:, data]).astype(BF16)
            v_ref[0, h] = jnp.where(keep, v[:, data], cv_ref[:, data]).astype(BF16)

    ksq = jnp.dot((k * k).astype(BF16), hsum_ref[...], preferred_element_type=F32)
    qsq = jnp.dot((q * q).astype(BF16), hsum_ref[...], preferred_element_type=F32)
    f2 = fcum * LOG2E
    rows = [jnp.max(ksq, axis=0, keepdims=True), jnp.max(qsq, axis=0, keepdims=True),
            f2[tm - 1:tm, :], f2[0:1, :]]
    stats_ref[0] = jnp.concatenate(rows + [jnp.zeros((STATS_ROWS - len(rows), HEAD_LANES), F32)], axis=0)


def _qkv(x2d, batch, seq, gq, gkv, wq, wk, wv, wf, bf, pq, pk, cq, ck, cv, hsum):
    n_tok, d = x2d.shape
    tm = TM_TOKENS
    tiles_per_seq = seq // tm
    wide = d
    head_spec = pl.BlockSpec((1, N_HEADS, tm, HEAD_LANES),
                             lambda i: (i // tiles_per_seq, 0, i % tiles_per_seq, 0))
    head_shape = jax.ShapeDtypeStruct((batch, N_HEADS, seq, HEAD_LANES), BF16)
    return pl.pallas_call(
        functools.partial(_qkv_kernel, tiles_per_seq=tiles_per_seq),
        grid=(n_tok // tm,),
        in_specs=[
            pl.BlockSpec((tm, d), lambda i: (i, 0)),
            _const_spec((1, d)), _const_spec((1, d)),
            _const_spec((d, d)), _const_spec((d, d)), _const_spec((d, d)),
            _const_spec((d, HEAD_LANES)), _const_spec((1, HEAD_LANES)),
            _const_spec((HEAD_LANES, wide)), _const_spec((HEAD_LANES, wide)),
            _const_spec((1, wide)), _const_spec((1, wide)), _const_spec((1, wide)),
            _const_spec((d, HEAD_LANES)),
        ],
        out_specs=[head_spec, head_spec, head_spec,
                   pl.BlockSpec((1, STATS_ROWS, HEAD_LANES), lambda i: (i, 0, 0))],
        out_shape=[head_shape, head_shape, head_shape,
                   jax.ShapeDtypeStruct((n_tok // tm, STATS_ROWS, HEAD_LANES), F32)],
        scratch_shapes=[pltpu.VMEM((1, HEAD_LANES), F32)],
        compiler_params=_params("arbitrary"),
        name="qkv",
    )(x2d, gq, gkv, wq, wk, wv, wf, bf, pq, pk, cq, ck, cv, hsum)


def _attn_kernel(q_ref, k_ref, v_ref, kstat_ref, qstat_ref, o_ref, p_buf, corr_buf, m_buf, acc_buf, *, tk):
    tq = q_ref.shape[2]
    qi = pl.program_id(2)
    chunks_per_q = tq // tk
    assert chunks_per_q == 2
    n_below = qi * chunks_per_q
    heads = range(2)
    tri = (lax.broadcasted_iota(jnp.int32, (tk, tk), 0) >= lax.broadcasted_iota(jnp.int32, (tk, tk), 1))
    lane = lax.broadcasted_iota(jnp.int32, (1, HEAD_LANES), 1)

    def scores(hh, j, rows=slice(None)):
        k = k_ref[0, hh, pl.ds(pl.multiple_of(j * tk, tk), tk), :]
        return lax.dot_general(q_ref[0, hh, rows, :], k, (((1,), (1,)), ((), ())),
                               preferred_element_type=F32)

    def mask_top(s):
        masked = jnp.where(tri, s[:tk], NEG_INF)
        return masked if s.shape[0] == tk else jnp.concatenate([masked, s[tk:]], axis=0)

    def softmax_chunk(s, m):
        blocks = [s[:, c:c + HEAD_LANES] for c in range(0, s.shape[1], HEAD_LANES)]
        m_new = jnp.maximum(m, jnp.max(functools.reduce(jnp.maximum, blocks), axis=-1, keepdims=True))
        p = jnp.concatenate([jnp.exp2(b - m_new) for b in blocks], axis=1)
        return m_new, p.astype(BF16), jnp.exp2(m - m_new)

    def value_matmul(hh, j, p):
        v = v_ref[0, hh, pl.ds(pl.multiple_of(j * tk, tk), tk), :]
        return jnp.dot(p, v, preferred_element_type=F32)

    def prepare(hh, j, slot, valid=None):
        m_old = m_buf[hh]
        m, p, corr = softmax_chunk(scores(hh, j), m_old)
        m_buf[hh] = m if valid is None else jnp.where(valid, m, m_old)
        p_buf[hh, slot] = p
        corr_buf[hh, slot] = corr

    def consume(hh, j, slot):
        acc_buf[hh] = acc_buf[hh] * corr_buf[hh, slot] + value_matmul(hh, j, p_buf[hh, slot])

    first_needed = []
    for hh in heads:
        m, p, _ = softmax_chunk(mask_top(scores(hh, n_below)), jnp.full((tq, HEAD_LANES), NEG_INF, F32))
        acc = value_matmul(hh, n_below, p)
        m_low, p, corr = softmax_chunk(mask_top(scores(hh, n_below + 1, rows=slice(tk, tq))), m[tk:])
        m = jnp.concatenate([m[:tk], m_low], axis=0)
        m_buf[hh] = m
        acc_buf[hh] = jnp.concatenate([acc[:tk], acc[tk:] * corr + value_matmul(hh, n_below + 1, p)], axis=0)

        kst = kstat_ref[0, hh]
        qst = qstat_ref[0, hh]
        mine = lane == qi
        qn = jnp.sum(jnp.where(mine, qst[0:1], 0.0), axis=1, keepdims=True)
        f0 = jnp.sum(jnp.where(mine, qst[1:2], 0.0), axis=1, keepdims=True)
        bound = qn * kst[0:1] * NORM_SLACK + (f0 - kst[1:2])
        floor = jnp.min(m, axis=0, keepdims=True) - SKIP_LOG2
        dead = (bound < floor) & (lane < n_below)
        first_needed.append(jnp.sum(dead.astype(jnp.int32)))

    starts = [f // 2 * 2 for f in first_needed]
    joint_start = jnp.maximum(starts[0], starts[1])

    for hh in heads:
        prepare(hh, jnp.maximum(n_below - 1, 0), 0, valid=qi > 0)

    def pair_body(which, top):
        def body(jj, _):
            newest = top - 1 - 2 * jj
            for slot in range(2):
                for hh in which:
                    prepare(hh, jnp.maximum(newest - slot - 1, 0), 1 - slot)
                    consume(hh, newest - slot, slot)
            return 0
        return body

    lax.fori_loop(0, (n_below - joint_start) // 2, pair_body(heads, n_below), 0)
    for hh in heads:
        lax.fori_loop(0, (joint_start - starts[hh]) // 2, pair_body((hh,), joint_start), 0)
    outs = [acc_buf[hh] for hh in heads]
    even = outs[0] / outs[0][:, HEAD_DIM:HEAD_DIM + 1]
    odd = outs[1] / outs[1][:, 0:1]
    lane = lax.broadcasted_iota(jnp.int32, even.shape, 1)
    o_ref[0] = jnp.where(lane < HEAD_DIM, even, odd).astype(o_ref.dtype)


def _attention(q, k, v, kstat, qstat):
    batch, _, seq, _ = q.shape
    tq, tk = TQ_ATTN, TK_ATTN
    pair_kv = pl.BlockSpec((1, 2, seq, HEAD_LANES), lambda b, hp, i: (b, hp, 0, 0))
    pair_stat = pl.BlockSpec((1, 2, STATS_ROWS, HEAD_LANES), lambda b, hp, i: (b, hp, 0, 0))
    return pl.pallas_call(
        functools.partial(_attn_kernel, tk=tk),
        grid=(batch, N_HEADS // 2, seq // tq),
        in_specs=[pl.BlockSpec((1, 2, tq, HEAD_LANES), lambda b, hp, i: (b, hp, i, 0)), pair_kv, pair_kv,
                  pair_stat, pair_stat],
        out_specs=pl.BlockSpec((1, tq, HEAD_LANES), lambda b, hp, i: (b, i, hp)),
        out_shape=jax.ShapeDtypeStruct((batch, seq, D_MODEL), BF16),
        scratch_shapes=[pltpu.VMEM((2, 2, tq, tk), BF16),
                        pltpu.VMEM((2, 2, tq, HEAD_LANES), F32),
                        pltpu.VMEM((2, tq, HEAD_LANES), F32),
                        pltpu.VMEM((2, tq, HEAD_LANES), F32)],
        compiler_params=_params("arbitrary", "arbitrary", "arbitrary"),
        name="attn",
    )(q, k, v, kstat, qstat)


META_LANES = 8


def _oproj_kernel(o_ref, x_ref, wo_ref, g_ref, wrh_ref, wrl_ref,
                  x3_ref, h3_ref, meta_ref, cnt_ref, carry_ref):
    tm = x_ref.shape[0]

    @pl.when(pl.program_id(0) == 0)
    def _():
        carry_ref[...] = jnp.zeros_like(carry_ref)

    x3 = x_ref[...] + jnp.dot(o_ref[...], wo_ref[...], preferred_element_type=F32)
    x3_ref[...] = x3
    h3 = _rms(x3) * g_ref[...]
    h_hi = h3.astype(BF16)
    h_lo = (h3 - h_hi.astype(F32)).astype(BF16)
    h3_ref[...] = h_hi
    nt = (((1,), (1,)), ((), ()))
    logits = (lax.dot_general(wrh_ref[...], h_hi, nt, preferred_element_type=F32)
              + lax.dot_general(wrh_ref[...], h_lo, nt, preferred_element_type=F32)
              + lax.dot_general(wrl_ref[...], h_hi, nt, preferred_element_type=F32))
    lg = logits[:N_EXPERTS]
    expert = lax.broadcasted_iota(jnp.int32, lg.shape, 0)
    m1 = jnp.max(lg, axis=0, keepdims=True)
    i1 = jnp.min(jnp.where(lg == m1, expert, N_EXPERTS), axis=0, keepdims=True)
    lg2 = jnp.where(expert == i1, -jnp.inf, lg)
    m2 = jnp.max(lg2, axis=0, keepdims=True)
    i2 = jnp.min(jnp.where(lg2 == m2, expert, N_EXPERTS), axis=0, keepdims=True)
    e2 = jnp.exp(m2 - m1)
    w1 = 1.0 / (1.0 + e2)
    w2 = e2 / (1.0 + e2)
    sel1 = expert == i1
    sel2 = expert == i2
    chosen = jnp.where(sel1 | sel2, 1.0, 0.0)
    chosen16 = jnp.concatenate([chosen, jnp.zeros_like(chosen)], axis=0).astype(BF16)
    row_t = lax.broadcasted_iota(jnp.int32, (tm, tm), 0)
    col_t = lax.broadcasted_iota(jnp.int32, (tm, tm), 1)
    earlier = jnp.where(row_t < col_t, 1.0, 0.0).astype(BF16)
    before = carry_ref[...][:, 0:1] + jnp.dot(chosen16, earlier, preferred_element_type=F32)[:N_EXPERTS]
    carry_ref[...] = carry_ref[...] + jnp.sum(chosen, axis=1, keepdims=True)
    cnt_ref[...] = carry_ref[...]
    r1 = jnp.sum(jnp.where(sel1, before, 0.0), axis=0, keepdims=True)
    r2 = jnp.sum(jnp.where(sel2, before, 0.0), axis=0, keepdims=True)
    meta_t = jnp.concatenate([i1.astype(F32), i2.astype(F32), r1, r2, w1, w2,
                              jnp.zeros((HEAD_LANES - 6, tm), F32)], axis=0)
    meta_ref[...] = meta_t.T[:, :META_LANES]


def _oproj(o2d, x2d, wo, g, wr_hi, wr_lo):
    n_tok, d = x2d.shape
    tm = TM_TOKENS
    row = lambda i: (i, 0)
    return pl.pallas_call(
        _oproj_kernel,
        grid=(n_tok // tm,),
        in_specs=[
            pl.BlockSpec((tm, d), row), pl.BlockSpec((tm, d), row),
            _const_spec((d, d)), _const_spec((1, d)),
            _const_spec(wr_hi.shape), _const_spec(wr_lo.shape),
        ],
        out_specs=[pl.BlockSpec((tm, d), row), pl.BlockSpec((tm, d), row),
                   pl.BlockSpec((tm, META_LANES), row),
                   pl.BlockSpec((N_EXPERTS, HEAD_LANES), lambda i: (0, 0))],
        out_shape=[jax.ShapeDtypeStruct((n_tok, d), F32), jax.ShapeDtypeStruct((n_tok, d), BF16),
                   jax.ShapeDtypeStruct((n_tok, META_LANES), F32),
                   jax.ShapeDtypeStruct((N_EXPERTS, HEAD_LANES), F32)],
        scratch_shapes=[pltpu.VMEM((N_EXPERTS, HEAD_LANES), F32)],
        compiler_params=_params("arbitrary"),
        name="oproj",
    )(o2d, x2d, wo, g, wr_hi, wr_lo)


def _moe_kernel(te_ref, nu_ref, h_ref, wg_ref, wu_ref, wd_ref, y_ref):
    @pl.when(pl.program_id(1) == 0)
    def _():
        y_ref[...] = jnp.zeros_like(y_ref)

    @pl.when(pl.program_id(0) < nu_ref[0])
    def _():
        h = h_ref[...]
        for s0 in range(0, wg_ref.shape[2], FF_SUB):
            sub = slice(s0, s0 + FF_SUB)
            gate = jnp.dot(h, wg_ref[0, :, sub].astype(BF16), preferred_element_type=F32)
            up = jnp.dot(h, wu_ref[0, :, sub].astype(BF16), preferred_element_type=F32)
            act = (gate * jax.nn.sigmoid(gate) * up).astype(BF16)
            y_ref[...] += jnp.dot(act, wd_ref[0, sub, :].astype(BF16), preferred_element_type=F32)


def _moe(tile_expert, n_used, h_sorted, wg, wu, wd):
    n_slots, d = h_sorted.shape
    tm = TM_EXPERT
    fc = FF_CHUNK
    n_chunks = wg.shape[2] // fc
    def row_map(i, c, te, nu):
        return (jnp.minimum(i, nu[0] - 1), 0)
    def chunk(i, c, te, nu):
        return jnp.where(i < nu[0], c, n_chunks - 1)
    return pl.pallas_call(
        _moe_kernel,
        grid_spec=pltpu.PrefetchScalarGridSpec(
            num_scalar_prefetch=2,
            grid=(n_slots // tm, n_chunks),
            in_specs=[
                pl.BlockSpec((tm, d), row_map),
                pl.BlockSpec((1, d, fc), lambda i, c, te, nu: (te[i], 0, chunk(i, c, te, nu))),
                pl.BlockSpec((1, d, fc), lambda i, c, te, nu: (te[i], 0, chunk(i, c, te, nu))),
                pl.BlockSpec((1, fc, d), lambda i, c, te, nu: (te[i], chunk(i, c, te, nu), 0)),
            ],
            out_specs=pl.BlockSpec((tm, d), lambda i, c, te, nu: (i, 0)),
        ),
        out_shape=jax.ShapeDtypeStruct((n_slots, d), F32),
        compiler_params=_params("arbitrary", "arbitrary"),
        name="moe",
    )(tile_expert, n_used, h_sorted, wg, wu, wd)


def _final_kernel(x_ref, ya_ref, yb_ref, meta_ref, g_ref, o_ref):
    meta = meta_ref[...]
    x = x_ref[...] + meta[:, 4:5] * ya_ref[...] + meta[:, 5:6] * yb_ref[...]
    o_ref[...] = _rms(x) * g_ref[...]


def _final(x2d, ya, yb, meta, g):
    n_tok, d = x2d.shape
    tm = TM_TOKENS
    row = lambda i: (i, 0)
    big = pl.BlockSpec((tm, d), row)
    return pl.pallas_call(
        _final_kernel,
        grid=(n_tok // tm,),
        in_specs=[big, big, big, pl.BlockSpec((tm, META_LANES), row), _const_spec((1, d))],
        out_specs=big,
        out_shape=jax.ShapeDtypeStruct((n_tok, d), F32),
        compiler_params=_params("arbitrary"),
        name="final",
    )(x2d, ya, yb, meta, g)


def _gate_constants():
    wide = D_MODEL
    pq = np.zeros((HEAD_LANES, wide), np.float32)
    pk = np.zeros((HEAD_LANES, wide), np.float32)
    cq = np.zeros((1, wide), np.float32)
    ck = np.zeros((1, wide), np.float32)
    cv = np.zeros((1, wide), np.float32)
    for h in range(N_HEADS):
        base = (h // 2) * HEAD_LANES + (HEAD_DIM if h % 2 == 0 else 0)
        cv[0, base] = 1.0
        for piece in range(N_PIECES):
            r = piece * N_HEADS + h
            pq[r, base + piece] = 1.0
            ck[0, base + piece] = 1.0
            cq[0, base + N_PIECES + piece] = 1.0
            pk[r, base + N_PIECES + piece] = -1.0
    hsum = np.zeros((D_MODEL, HEAD_LANES), np.float32)
    hsum[np.arange(D_MODEL), np.arange(D_MODEL) // HEAD_DIM] = 1.0
    return (jnp.asarray(pq, BF16), jnp.asarray(pk, BF16), jnp.asarray(cq), jnp.asarray(ck), jnp.asarray(cv),
            jnp.asarray(hsum, BF16))


def _skip_tables(stats, batch, seq):
    tiles = seq // TM_TOKENS
    per_q = TQ_ATTN // TM_TOKENS
    st = stats.reshape(batch, tiles, STATS_ROWS, HEAD_LANES)[..., :N_HEADS]
    st = jnp.swapaxes(st, 1, 3)
    knorm = lax.cummax(jnp.sqrt(st[:, :, 0]), axis=2)
    qnorm = jnp.sqrt(jnp.max(st[:, :, 1].reshape(batch, N_HEADS, tiles // per_q, per_q), axis=-1))
    f_start = st[:, :, 3].reshape(batch, N_HEADS, tiles // per_q, per_q)[..., 0]

    def table(row0, row1):
        t = jnp.stack([row0, row1], axis=2)
        return jnp.pad(t, ((0, 0), (0, 0), (0, STATS_ROWS - 2), (0, HEAD_LANES - t.shape[-1])))

    return table(knorm, st[:, :, 2]), table(qnorm, f_start)


def kernel(x, norm_mix, norm_ffn, pool_w, pool_scale, norm_kv, w_kv, b_f, w_q, w_o, ffn_gate, ffn_up,
           ffn_down, w_router, exp_gate, exp_up, exp_down, norm_final):
    batch, seq, d = x.shape
    n_tok = batch * seq
    assert d == D_MODEL and seq % TQ_ATTN == 0 and TQ_ATTN == 2 * TK_ATTN and TK_ATTN == TM_TOKENS
    assert seq // TK_ATTN <= HEAD_LANES
    row = lambda v: v.reshape(1, -1).astype(F32)
    x2d = x.reshape(n_tok, d)

    x2 = _layer0(x2d, seq, row(norm_mix[0]), row(norm_ffn[0]), pool_w[0].astype(BF16), row(pool_scale[0]),
                 ffn_gate[0].astype(BF16), ffn_up[0].astype(BF16), ffn_down[0].astype(BF16))

    wq = w_q[0].astype(BF16)
    wk = w_kv[:, :d].astype(BF16)
    wv = w_kv[:, d:2 * d].astype(BF16)
    wf = jnp.zeros((d, HEAD_LANES), F32).at[:, :N_PIECES * N_HEADS].set(
        jnp.tile(w_kv[:, 2 * d:], (1, N_PIECES))).astype(BF16)
    bf = jnp.zeros((1, HEAD_LANES), F32).at[0, :N_PIECES * N_HEADS].set(jnp.tile(b_f.astype(F32), N_PIECES))
    q, k, v, stats = _qkv(x2, batch, seq, row(norm_mix[1]), row(norm_kv), wq, wk, wv, wf, bf,
                          *_gate_constants())

    o = _attention(q, k, v, *_skip_tables(stats, batch, seq))

    wr = jnp.zeros((2 * N_EXPERTS, d), F32).at[:N_EXPERTS].set(w_router[0].T)
    wr_hi = wr.astype(BF16)
    wr_lo = (wr - wr_hi.astype(F32)).astype(BF16)
    x3, h3, meta, counts = _oproj(o.reshape(n_tok, d), x2, w_o[0].astype(BF16), row(norm_ffn[1]), wr_hi, wr_lo)

    tm = TM_EXPERT
    n_tiles = n_tok * TOP_K // tm + N_EXPERTS
    counts = counts[:, 0].astype(jnp.int32)
    padded = (counts + tm - 1) // tm * tm
    group_end = jnp.cumsum(padded)
    group_start = group_end - padded
    expert = meta[:, 0:2].astype(jnp.int32)
    slot = group_start[expert] + meta[:, 2:4].astype(jnp.int32)
    n_used = (group_end[-1] // tm).astype(jnp.int32).reshape(1)
    tile_expert = jnp.sum(jnp.arange(n_tiles)[:, None] * tm >= group_end[None, :], axis=1)
    last_expert = jnp.max(jnp.where(counts > 0, jnp.arange(N_EXPERTS), 0))
    tile_expert = jnp.minimum(tile_expert, last_expert).astype(jnp.int32)
    src = (jnp.arange(n_tiles * tm, dtype=jnp.int32) % n_tok).at[slot.reshape(-1)].set(
        jnp.repeat(jnp.arange(n_tok, dtype=jnp.int32), TOP_K), unique_indices=True, mode="promise_in_bounds")
    h_sorted = jnp.take(h3, src, axis=0, mode="clip")

    y = _moe(tile_expert, n_used, h_sorted, exp_gate[0], exp_up[0], exp_down[0])

    ya = jnp.take(y, slot[:, 0], axis=0, mode="clip")
    yb = jnp.take(y, slot[:, 1], axis=0, mode="clip")
    out = _final(x3, ya, yb, meta, row(norm_final))
    return out.reshape(batch, seq, d)
```

```python
import functools

import jax
import jax.numpy as jnp
import numpy as np
from jax import lax
from jax.experimental import pallas as pl
from jax.experimental.pallas import tpu as pltpu

F32 = jnp.float32
BF16 = jnp.bfloat16

D_MODEL = 1024
HEAD_DIM = 64
N_HEADS = D_MODEL // HEAD_DIM
HEAD_LANES = 128
POOL_WINDOWS = (2, 4, 8, 16)
POOL_GROUP = D_MODEL // len(POOL_WINDOWS)
POOL_HALO = 16
N_EXPERTS = 8
TOP_K = 2
EPS = 1e-5
NEG_INF = -1e30
LOG2E = 1.4426950408889634
STATS_ROWS = 8
SKIP_LOG2 = 152.0
NORM_SLACK = 1.02

N_PIECES = 3

TM_TOKENS = 512
TQ_ATTN = 1024
TK_ATTN = 512
TM_EXPERT = 1024
FF_CHUNK = 512
FF_SUB = 256

VMEM_LIMIT_BYTES = 56 * 1024 * 1024


def _params(*sem, flags=None):
    return pltpu.CompilerParams(dimension_semantics=sem, vmem_limit_bytes=VMEM_LIMIT_BYTES, flags=flags)


def _const_spec(shape):
    zeros = (0,) * len(shape)
    return pl.BlockSpec(shape, lambda *_: zeros, pipeline_mode=pl.Buffered(1))


def _rms(v):
    return v * lax.rsqrt(jnp.mean(v * v, axis=-1, keepdims=True) + EPS)


def _split3(v):
    hi = v.astype(BF16)
    r1 = v - hi.astype(F32)
    mid = r1.astype(BF16)
    lo = (r1 - mid.astype(F32)).astype(BF16)
    return hi, mid, lo


def _lower_tri(n, strict):
    r = lax.broadcasted_iota(jnp.int32, (n, n), 0)
    c = lax.broadcasted_iota(jnp.int32, (n, n), 1)
    return jnp.where((c < r) if strict else (c <= r), 1.0, 0.0).astype(BF16)


def _layer0_kernel(x_ref, xp_ref, gmix_ref, gffn_ref, pw_ref, ps_ref, wg_ref, wu_ref, wd_ref,
                   o_ref, *, tiles_per_seq):
    tm = x_ref.shape[0]
    t = pl.program_id(0) % tiles_per_seq
    x = x_ref[...]
    gmix = gmix_ref[...]
    h = _rms(x) * gmix
    hp = jnp.where(t == 0, 0.0, _rms(xp_ref[...]) * gmix)
    ext = jnp.concatenate([hp, h], axis=0)
    pos = t * tm + lax.broadcasted_iota(jnp.int32, (tm, 1), 0)
    mixed = []
    for g, w in enumerate(POOL_WINDOWS):
        lanes = slice(g * POOL_GROUP, (g + 1) * POOL_GROUP)
        win = ext[:, lanes]
        shift = 1
        while shift < w:
            win = win + pltpu.roll(win, shift, axis=0)
            shift *= 2
        cnt = jnp.minimum(pos + 1, w).astype(F32)
        pooled = win[POOL_HALO:, :] / cnt - h[:, lanes]
        mixed.append(jnp.dot(pooled.astype(BF16), pw_ref[g], preferred_element_type=F32))
    x1 = x + jnp.concatenate(mixed, axis=1) * ps_ref[...]
    h2 = (_rms(x1) * gffn_ref[...]).astype(BF16)
    gate = jnp.dot(h2, wg_ref[...], preferred_element_type=F32)
    up = jnp.dot(h2, wu_ref[...], preferred_element_type=F32)
    act = (gate * jax.nn.sigmoid(gate) * up).astype(BF16)
    o_ref[...] = x1 + jnp.dot(act, wd_ref[...], preferred_element_type=F32)


def _layer0(x2d, seq, gmix, gffn, pool_w, pool_scale, wg, wu, wd):
    n_tok, d = x2d.shape
    tm = TM_TOKENS
    tiles_per_seq = seq // tm
    halo_blocks = tm // POOL_HALO
    d_ff = wg.shape[1]
    return pl.pallas_call(
        functools.partial(_layer0_kernel, tiles_per_seq=tiles_per_seq),
        grid=(n_tok // tm,),
        in_specs=[
            pl.BlockSpec((tm, d), lambda i: (i, 0)),
            pl.BlockSpec((POOL_HALO, d), lambda i: (jnp.maximum(i * halo_blocks - 1, 0), 0)),
            _const_spec((1, d)), _const_spec((1, d)),
            _const_spec(pool_w.shape), _const_spec((1, d)),
            _const_spec((d, d_ff)), _const_spec((d, d_ff)), _const_spec((d_ff, d)),
        ],
        out_specs=pl.BlockSpec((tm, d), lambda i: (i, 0)),
        out_shape=jax.ShapeDtypeStruct((n_tok, d), F32),
        compiler_params=_params("arbitrary"),
        name="layer0",
    )(x2d, x2d, gmix, gffn, pool_w, pool_scale, wg, wu, wd)


def _qkv_kernel(x_ref, gq_ref, gkv_ref, wq_ref, wk_ref, wv_ref, wf_ref, bf_ref, pq_ref, pk_ref,
                cq_ref, ck_ref, cv_ref, hsum_ref, q_ref, k_ref, v_ref, stats_ref, carry_ref, *, tiles_per_seq):
    tm = x_ref.shape[0]

    @pl.when(pl.program_id(0) % tiles_per_seq == 0)
    def _():
        carry_ref[...] = jnp.zeros_like(carry_ref)

    xn = _rms(x_ref[...])
    hq = (xn * gq_ref[...]).astype(BF16)
    hkv = (xn * gkv_ref[...]).astype(BF16)

    z = jnp.dot(hkv, wf_ref[...], preferred_element_type=F32) + bf_ref[...]
    logf = jnp.minimum(z, 0.0) - jnp.log1p(jnp.exp(-jnp.abs(z)))
    tri = _lower_tri(tm, strict=False)
    fcum = carry_ref[...] + sum(jnp.dot(tri, p, preferred_element_type=F32) for p in _split3(logf))
    carry_ref[...] = fcum[tm - 1:tm, :]
    hi, mid, lo = _split3(fcum * LOG2E)
    lane = lax.broadcasted_iota(jnp.int32, fcum.shape, 1)
    pieces = jnp.where(lane < N_HEADS, hi, jnp.where(lane < 2 * N_HEADS, mid, lo))

    q = jnp.dot(hq, wq_ref[...], preferred_element_type=F32) * (HEAD_DIM ** -0.5 * LOG2E)
    k = jnp.dot(hkv, wk_ref[...], preferred_element_type=F32)
    v = jnp.dot(hkv, wv_ref[...], preferred_element_type=F32)
    gq = jnp.dot(pieces, pq_ref[...], preferred_element_type=F32) + cq_ref[...]
    gk = jnp.dot(pieces, pk_ref[...], preferred_element_type=F32) + ck_ref[...]
    low = lax.broadcasted_iota(jnp.int32, (tm, HEAD_LANES), 1) < HEAD_DIM
    for pair in range(N_HEADS // 2):
        data = slice(pair * HEAD_LANES, (pair + 1) * HEAD_LANES)
        for odd in range(2):
            h = 2 * pair + odd
            keep = jnp.logical_not(low) if odd else low
            q_ref[0, h] = jnp.where(keep, q[:, data], gq[:, data]).astype(BF16)
            k_ref[0, h] = jnp.where(keep, k[:, data], gk[:, data]).astype(BF16)
            v_ref[0, h] = jnp.where(keep, v[:, data], cv_ref[:, data]).astype(BF16)

    ksq = jnp.dot((k * k).astype(BF16), hsum_ref[...], preferred_element_type=F32)
    qsq = jnp.dot((q * q).astype(BF16), hsum_ref[...], preferred_element_type=F32)
    f2 = fcum * LOG2E
    rows = [jnp.max(ksq, axis=0, keepdims=True), jnp.max(qsq, axis=0, keepdims=True),
            f2[tm - 1:tm, :], f2[0:1, :]]
    stats_ref[0] = jnp.concatenate(rows + [jnp.zeros((STATS_ROWS - len(rows), HEAD_LANES), F32)], axis=0)


def _qkv(x2d, batch, seq, gq, gkv, wq, wk, wv, wf, bf, pq, pk, cq, ck, cv, hsum):
    n_tok, d = x2d.shape
    tm = TM_TOKENS
    tiles_per_seq = seq // tm
    wide = d
    head_spec = pl.BlockSpec((1, N_HEADS, tm, HEAD_LANES),
                             lambda i: (i // tiles_per_seq, 0, i % tiles_per_seq, 0))
    head_shape = jax.ShapeDtypeStruct((batch, N_HEADS, seq, HEAD_LANES), BF16)
    return pl.pallas_call(
        functools.partial(_qkv_kernel, tiles_per_seq=tiles_per_seq),
        grid=(n_tok // tm,),
        in_specs=[
            pl.BlockSpec((tm, d), lambda i: (i, 0)),
            _const_spec((1, d)), _const_spec((1, d)),
            _const_spec((d, d)), _const_spec((d, d)), _const_spec((d, d)),
            _const_spec((d, HEAD_LANES)), _const_spec((1, HEAD_LANES)),
            _const_spec((HEAD_LANES, wide)), _const_spec((HEAD_LANES, wide)),
            _const_spec((1, wide)), _const_spec((1, wide)), _const_spec((1, wide)),
            _const_spec((d, HEAD_LANES)),
        ],
        out_specs=[head_spec, head_spec, head_spec,
                   pl.BlockSpec((1, STATS_ROWS, HEAD_LANES), lambda i: (i, 0, 0))],
        out_shape=[head_shape, head_shape, head_shape,
                   jax.ShapeDtypeStruct((n_tok // tm, STATS_ROWS, HEAD_LANES), F32)],
        scratch_shapes=[pltpu.VMEM((1, HEAD_LANES), F32)],
        compiler_params=_params("arbitrary"),
        name="qkv",
    )(x2d, gq, gkv, wq, wk, wv, wf, bf, pq, pk, cq, ck, cv, hsum)


def _attn_kernel(q_ref, k_ref, v_ref, kstat_ref, qstat_ref, o_ref, p_buf, corr_buf, m_buf, acc_buf, *, tk):
    tq = q_ref.shape[2]
    qi = pl.program_id(2)
    chunks_per_q = tq // tk
    assert chunks_per_q == 2
    n_below = qi * chunks_per_q
    heads = range(2)
    tri = (lax.broadcasted_iota(jnp.int32, (tk, tk), 0) >= lax.broadcasted_iota(jnp.int32, (tk, tk), 1))
    lane = lax.broadcasted_iota(jnp.int32, (1, HEAD_LANES), 1)

    def scores(hh, j, rows=slice(None)):
        k = k_ref[0, hh, pl.ds(pl.multiple_of(j * tk, tk), tk), :]
        return lax.dot_general(q_ref[0, hh, rows, :], k, (((1,), (1,)), ((), ())),
                               preferred_element_type=F32)

    def mask_top(s):
        masked = jnp.where(tri, s[:tk], NEG_INF)
        return masked if s.shape[0] == tk else jnp.concatenate([masked, s[tk:]], axis=0)

    def softmax_chunk(s, m):
        blocks = [s[:, c:c + HEAD_LANES] for c in range(0, s.shape[1], HEAD_LANES)]
        m_new = jnp.maximum(m, jnp.max(functools.reduce(jnp.maximum, blocks), axis=-1, keepdims=True))
        p = jnp.concatenate([jnp.exp2(b - m_new) for b in blocks], axis=1)
        return m_new, p.astype(BF16), jnp.exp2(m - m_new)

    def value_matmul(hh, j, p):
        v = v_ref[0, hh, pl.ds(pl.multiple_of(j * tk, tk), tk), :]
        return jnp.dot(p, v, preferred_element_type=F32)

    def prepare(hh, j, slot, valid=None):
        m_old = m_buf[hh]
        m, p, corr = softmax_chunk(scores(hh, j), m_old)
        m_buf[hh] = m if valid is None else jnp.where(valid, m, m_old)
        p_buf[hh, slot] = p
        corr_buf[hh, slot] = corr

    def consume(hh, j, slot):
        acc_buf[hh] = acc_buf[hh] * corr_buf[hh, slot] + value_matmul(hh, j, p_buf[hh, slot])

    first_needed = []
    for hh in heads:
        m, p, _ = softmax_chunk(mask_top(scores(hh, n_below)), jnp.full((tq, HEAD_LANES), NEG_INF, F32))
        acc = value_matmul(hh, n_below, p)
        m_low, p, corr = softmax_chunk(mask_top(scores(hh, n_below + 1, rows=slice(tk, tq))), m[tk:])
        m = jnp.concatenate([m[:tk], m_low], axis=0)
        m_buf[hh] = m
        acc_buf[hh] = jnp.concatenate([acc[:tk], acc[tk:] * corr + value_matmul(hh, n_below + 1, p)], axis=0)

        kst = kstat_ref[0, hh]
        qst = qstat_ref[0, hh]
        mine = lane == qi
        qn = jnp.sum(jnp.where(mine, qst[0:1], 0.0), axis=1, keepdims=True)
        f0 = jnp.sum(jnp.where(mine, qst[1:2], 0.0), axis=1, keepdims=True)
        bound = qn * kst[0:1] * NORM_SLACK + (f0 - kst[1:2])
        floor = jnp.min(m, axis=0, keepdims=True) - SKIP_LOG2
        dead = (bound < floor) & (lane < n_below)
        first_needed.append(jnp.sum(dead.astype(jnp.int32)))

    starts = [f // 2 * 2 for f in first_needed]
    joint_start = jnp.maximum(starts[0], starts[1])

    for hh in heads:
        prepare(hh, jnp.maximum(n_below - 1, 0), 0, valid=qi > 0)

    def pair_body(which, top):
        def body(jj, _):
            newest = top - 1 - 2 * jj
            for slot in range(2):
                for hh in which:
                    prepare(hh, jnp.maximum(newest - slot - 1, 0), 1 - slot)
                    consume(hh, newest - slot, slot)
            return 0
        return body

    lax.fori_loop(0, (n_below - joint_start) // 2, pair_body(heads, n_below), 0)
    for hh in heads:
        lax.fori_loop(0, (joint_start - starts[hh]) // 2, pair_body((hh,), joint_start), 0)
    outs = [acc_buf[hh] for hh in heads]
    even = outs[0] / outs[0][:, HEAD_DIM:HEAD_DIM + 1]
    odd = outs[1] / outs[1][:, 0:1]
    lane = lax.broadcasted_iota(jnp.int32, even.shape, 1)
    o_ref[0] = jnp.where(lane < HEAD_DIM, even, odd).astype(o_ref.dtype)


def _attention(q, k, v, kstat, qstat):
    batch, _, seq, _ = q.shape
    tq, tk = TQ_ATTN, TK_ATTN
    pair_kv = pl.BlockSpec((1, 2, seq, HEAD_LANES), lambda b, hp, i: (b, hp, 0, 0))
    pair_stat = pl.BlockSpec((1, 2, STATS_ROWS, HEAD_LANES), lambda b, hp, i: (b, hp, 0, 0))
    return pl.pallas_call(
        functools.partial(_attn_kernel, tk=tk),
        grid=(batch, N_HEADS // 2, seq // tq),
        in_specs=[pl.BlockSpec((1, 2, tq, HEAD_LANES), lambda b, hp, i: (b, hp, i, 0)), pair_kv, pair_kv,
                  pair_stat, pair_stat],
        out_specs=pl.BlockSpec((1, tq, HEAD_LANES), lambda b, hp, i: (b, i, hp)),
        out_shape=jax.ShapeDtypeStruct((batch, seq, D_MODEL), BF16),
        scratch_shapes=[pltpu.VMEM((2, 2, tq, tk), BF16),
                        pltpu.VMEM((2, 2, tq, HEAD_LANES), F32),
                        pltpu.VMEM((2, tq, HEAD_LANES), F32),
                        pltpu.VMEM((2, tq, HEAD_LANES), F32)],
        compiler_params=_params("arbitrary", "arbitrary", "arbitrary"),
        name="attn",
    )(q, k, v, kstat, qstat)


META_LANES = 8


def _oproj_kernel(o_ref, x_ref, wo_ref, g_ref, wrh_ref, wrl_ref,
                  x3_ref, h3_ref, meta_ref, cnt_ref, carry_ref):
    tm = x_ref.shape[0]

    @pl.when(pl.program_id(0) == 0)
    def _():
        carry_ref[...] = jnp.zeros_like(carry_ref)

    x3 = x_ref[...] + jnp.dot(o_ref[...], wo_ref[...], preferred_element_type=F32)
    x3_ref[...] = x3
    h3 = _rms(x3) * g_ref[...]
    h_hi = h3.astype(BF16)
    h_lo = (h3 - h_hi.astype(F32)).astype(BF16)
    h3_ref[...] = h_hi
    nt = (((1,), (1,)), ((), ()))
    logits = (lax.dot_general(wrh_ref[...], h_hi, nt, preferred_element_type=F32)
              + lax.dot_general(wrh_ref[...], h_lo, nt, preferred_element_type=F32)
              + lax.dot_general(wrl_ref[...], h_hi, nt, preferred_element_type=F32))
    lg = logits[:N_EXPERTS]
    expert = lax.broadcasted_iota(jnp.int32, lg.shape, 0)
    m1 = jnp.max(lg, axis=0, keepdims=True)
    i1 = jnp.min(jnp.where(lg == m1, expert, N_EXPERTS), axis=0, keepdims=True)
    lg2 = jnp.where(expert == i1, -jnp.inf, lg)
    m2 = jnp.max(lg2, axis=0, keepdims=True)
    i2 = jnp.min(jnp.where(lg2 == m2, expert, N_EXPERTS), axis=0, keepdims=True)
    e2 = jnp.exp(m2 - m1)
    w1 = 1.0 / (1.0 + e2)
    w2 = e2 / (1.0 + e2)
    sel1 = expert == i1
    sel2 = expert == i2
    chosen = jnp.where(sel1 | sel2, 1.0, 0.0)
    chosen16 = jnp.concatenate([chosen, jnp.zeros_like(chosen)], axis=0).astype(BF16)
    row_t = lax.broadcasted_iota(jnp.int32, (tm, tm), 0)
    col_t = lax.broadcasted_iota(jnp.int32, (tm, tm), 1)
    earlier = jnp.where(row_t < col_t, 1.0, 0.0).astype(BF16)
    before = carry_ref[...][:, 0:1] + jnp.dot(chosen16, earlier, preferred_element_type=F32)[:N_EXPERTS]
    carry_ref[...] = carry_ref[...] + jnp.sum(chosen, axis=1, keepdims=True)
    cnt_ref[...] = carry_ref[...]
    r1 = jnp.sum(jnp.where(sel1, before, 0.0), axis=0, keepdims=True)
    r2 = jnp.sum(jnp.where(sel2, before, 0.0), axis=0, keepdims=True)
    meta_t = jnp.concatenate([i1.astype(F32), i2.astype(F32), r1, r2, w1, w2,
                              jnp.zeros((HEAD_LANES - 6, tm), F32)], axis=0)
    meta_ref[...] = meta_t.T[:, :META_LANES]


def _oproj(o2d, x2d, wo, g, wr_hi, wr_lo):
    n_tok, d = x2d.shape
    tm = TM_TOKENS
    row = lambda i: (i, 0)
    return pl.pallas_call(
        _oproj_kernel,
        grid=(n_tok // tm,),
        in_specs=[
            pl.BlockSpec((tm, d), row), pl.BlockSpec((tm, d), row),
            _const_spec((d, d)), _const_spec((1, d)),
            _const_spec(wr_hi.shape), _const_spec(wr_lo.shape),
        ],
        out_specs=[pl.BlockSpec((tm, d), row), pl.BlockSpec((tm, d), row),
                   pl.BlockSpec((tm, META_LANES), row),
                   pl.BlockSpec((N_EXPERTS, HEAD_LANES), lambda i: (0, 0))],
        out_shape=[jax.ShapeDtypeStruct((n_tok, d), F32), jax.ShapeDtypeStruct((n_tok, d), BF16),
                   jax.ShapeDtypeStruct((n_tok, META_LANES), F32),
                   jax.ShapeDtypeStruct((N_EXPERTS, HEAD_LANES), F32)],
        scratch_shapes=[pltpu.VMEM((N_EXPERTS, HEAD_LANES), F32)],
        compiler_params=_params("arbitrary"),
        name="oproj",
    )(o2d, x2d, wo, g, wr_hi, wr_lo)


def _moe_kernel(te_ref, nu_ref, h_ref, wg_ref, wu_ref, wd_ref, y_ref):
    @pl.when(pl.program_id(1) == 0)
    def _():
        y_ref[...] = jnp.zeros_like(y_ref)

    @pl.when(pl.program_id(0) < nu_ref[0])
    def _():
        h = h_ref[...]
        for s0 in range(0, wg_ref.shape[2], FF_SUB):
            sub = slice(s0, s0 + FF_SUB)
            gate = jnp.dot(h, wg_ref[0, :, sub].astype(BF16), preferred_element_type=F32)
            up = jnp.dot(h, wu_ref[0, :, sub].astype(BF16), preferred_element_type=F32)
            act = (gate * jax.nn.sigmoid(gate) * up).astype(BF16)
            y_ref[...] += jnp.dot(act, wd_ref[0, sub, :].astype(BF16), preferred_element_type=F32)


def _moe(tile_expert, n_used, h_sorted, wg, wu, wd):
    n_slots, d = h_sorted.shape
    tm = TM_EXPERT
    fc = FF_CHUNK
    n_chunks = wg.shape[2] // fc
    def row_map(i, c, te, nu):
        return (jnp.minimum(i, nu[0] - 1), 0)
    def chunk(i, c, te, nu):
        return jnp.where(i < nu[0], c, n_chunks - 1)
    return pl.pallas_call(
        _moe_kernel,
        grid_spec=pltpu.PrefetchScalarGridSpec(
            num_scalar_prefetch=2,
            grid=(n_slots // tm, n_chunks),
            in_specs=[
                pl.BlockSpec((tm, d), row_map),
                pl.BlockSpec((1, d, fc), lambda i, c, te, nu: (te[i], 0, chunk(i, c, te, nu))),
                pl.BlockSpec((1, d, fc), lambda i, c, te, nu: (te[i], 0, chunk(i, c, te, nu))),
                pl.BlockSpec((1, fc, d), lambda i, c, te, nu: (te[i], chunk(i, c, te, nu), 0)),
            ],
            out_specs=pl.BlockSpec((tm, d), lambda i, c, te, nu: (i, 0)),
        ),
        out_shape=jax.ShapeDtypeStruct((n_slots, d), F32),
        compiler_params=_params("arbitrary", "arbitrary"),
        name="moe",
    )(tile_expert, n_used, h_sorted, wg, wu, wd)


def _final_kernel(x_ref, ya_ref, yb_ref, meta_ref, g_ref, o_ref):
    meta = meta_ref[...]
    x = x_ref[...] + meta[:, 4:5] * ya_ref[...] + meta[:, 5:6] * yb_ref[...]
    o_ref[...] = _rms(x) * g_ref[...]


def _final(x2d, ya, yb, meta, g):
    n_tok, d = x2d.shape
    tm = TM_TOKENS
    row = lambda i: (i, 0)
    big = pl.BlockSpec((tm, d), row)
    return pl.pallas_call(
        _final_kernel,
        grid=(n_tok // tm,),
        in_specs=[big, big, big, pl.BlockSpec((tm, META_LANES), row), _const_spec((1, d))],
        out_specs=big,
        out_shape=jax.ShapeDtypeStruct((n_tok, d), F32),
        compiler_params=_params("arbitrary"),
        name="final",
    )(x2d, ya, yb, meta, g)


def _gate_constants():
    wide = D_MODEL
    pq = np.zeros((HEAD_LANES, wide), np.float32)
    pk = np.zeros((HEAD_LANES, wide), np.float32)
    cq = np.zeros((1, wide), np.float32)
    ck = np.zeros((1, wide), np.float32)
    cv = np.zeros((1, wide), np.float32)
    for h in range(N_HEADS):
        base = (h // 2) * HEAD_LANES + (HEAD_DIM if h % 2 == 0 else 0)
        cv[0, base] = 1.0
        for piece in range(N_PIECES):
            r = piece * N_HEADS + h
            pq[r, base + piece] = 1.0
            ck[0, base + piece] = 1.0
            cq[0, base + N_PIECES + piece] = 1.0
            pk[r, base + N_PIECES + piece] = -1.0
    hsum = np.zeros((D_MODEL, HEAD_LANES), np.float32)
    hsum[np.arange(D_MODEL), np.arange(D_MODEL) // HEAD_DIM] = 1.0
    return (jnp.asarray(pq, BF16), jnp.asarray(pk, BF16), jnp.asarray(cq), jnp.asarray(ck), jnp.asarray(cv),
            jnp.asarray(hsum, BF16))


def _skip_tables(stats, batch, seq):
    tiles = seq // TM_TOKENS
    per_q = TQ_ATTN // TM_TOKENS
    st = stats.reshape(batch, tiles, STATS_ROWS, HEAD_LANES)[..., :N_HEADS]
    st = jnp.swapaxes(st, 1, 3)
    knorm = lax.cummax(jnp.sqrt(st[:, :, 0]), axis=2)
    qnorm = jnp.sqrt(jnp.max(st[:, :, 1].reshape(batch, N_HEADS, tiles // per_q, per_q), axis=-1))
    f_start = st[:, :, 3].reshape(batch, N_HEADS, tiles // per_q, per_q)[..., 0]

    def table(row0, row1):
        t = jnp.stack([row0, row1], axis=2)
        return jnp.pad(t, ((0, 0), (0, 0), (0, STATS_ROWS - 2), (0, HEAD_LANES - t.shape[-1])))

    return table(knorm, st[:, :, 2]), table(qnorm, f_start)


def kernel(x, norm_mix, norm_ffn, pool_w, pool_scale, norm_kv, w_kv, b_f, w_q, w_o, ffn_gate, ffn_up,
           ffn_down, w_router, exp_gate, exp_up, exp_down, norm_final):
    batch, seq, d = x.shape
    n_tok = batch * seq
    assert d == D_MODEL and seq % TQ_ATTN == 0 and TQ_ATTN == 2 * TK_ATTN and TK_ATTN == TM_TOKENS
    assert seq // TK_ATTN <= HEAD_LANES
    row = lambda v: v.reshape(1, -1).astype(F32)
    x2d = x.reshape(n_tok, d)

    x2 = _layer0(x2d, seq, row(norm_mix[0]), row(norm_ffn[0]), pool_w[0].astype(BF16), row(pool_scale[0]),
                 ffn_gate[0].astype(BF16), ffn_up[0].astype(BF16), ffn_down[0].astype(BF16))

    wq = w_q[0].astype(BF16)
    wk = w_kv[:, :d].astype(BF16)
    wv = w_kv[:, d:2 * d].astype(BF16)
    wf = jnp.zeros((d, HEAD_LANES), F32).at[:, :N_PIECES * N_HEADS].set(
        jnp.tile(w_kv[:, 2 * d:], (1, N_PIECES))).astype(BF16)
    bf = jnp.zeros((1, HEAD_LANES), F32).at[0, :N_PIECES * N_HEADS].set(jnp.tile(b_f.astype(F32), N_PIECES))
    q, k, v, stats = _qkv(x2, batch, seq, row(norm_mix[1]), row(norm_kv), wq, wk, wv, wf, bf,
                          *_gate_constants())

    o = _attention(q, k, v, *_skip_tables(stats, batch, seq))

    wr = jnp.zeros((2 * N_EXPERTS, d), F32).at[:N_EXPERTS].set(w_router[0].T)
    wr_hi = wr.astype(BF16)
    wr_lo = (wr - wr_hi.astype(F32)).astype(BF16)
    x3, h3, meta, counts = _oproj(o.reshape(n_tok, d), x2, w_o[0].astype(BF16), row(norm_ffn[1]), wr_hi, wr_lo)

    tm = TM_EXPERT
    n_tiles = n_tok * TOP_K // tm + N_EXPERTS
    counts = counts[:, 0].astype(jnp.int32)
    padded = (counts + tm - 1) // tm * tm
    group_end = jnp.cumsum(padded)
    group_start = group_end - padded
    expert = meta[:, 0:2].astype(jnp.int32)
    slot = group_start[expert] + meta[:, 2:4].astype(jnp.int32)
    n_used = (group_end[-1] // tm).astype(jnp.int32).reshape(1)
    tile_expert = jnp.sum(jnp.arange(n_tiles)[:, None] * tm >= group_end[None, :], axis=1)
    last_expert = jnp.max(jnp.where(counts > 0, jnp.arange(N_EXPERTS), 0))
    tile_expert = jnp.minimum(tile_expert, last_expert).astype(jnp.int32)
    order = jnp.argsort(slot.reshape(-1)).astype(jnp.int32)
    tile_ids = jnp.arange(n_tiles, dtype=jnp.int32)
    within = jnp.arange(tm, dtype=jnp.int32)[None, :]
    rank = (tile_ids * tm - group_start[tile_expert])[:, None] + within
    compact = (jnp.cumsum(counts) - counts)[tile_expert][:, None] + rank
    src = jnp.where(rank < counts[tile_expert][:, None],
                    order[jnp.minimum(compact, n_tok * TOP_K - 1)] // TOP_K,
                    (tile_ids[:, None] * tm + within) % n_tok).reshape(-1)
    h_sorted = jnp.take(h3, src, axis=0, mode="clip")

    y = _moe(tile_expert, n_used, h_sorted, exp_gate[0], exp_up[0], exp_down[0])

    ya = jnp.take(y, slot[:, 0], axis=0, mode="clip")
    yb = jnp.take(y, slot[:, 1], axis=0, mode="clip")
    out = _final(x3, ya, yb, meta, row(norm_final))
    return out.reshape(batch, seq, d)
```

```python
import functools

import jax
import jax.numpy as jnp
import numpy as np
from jax import lax
from jax.experimental import pallas as pl
from jax.experimental.pallas import tpu as pltpu

F32 = jnp.float32
BF16 = jnp.bfloat16

D_MODEL = 1024
HEAD_DIM = 64
N_HEADS = D_MODEL // HEAD_DIM
HEAD_LANES = 128
POOL_WINDOWS = (2, 4, 8, 16)
POOL_GROUP = D_MODEL // len(POOL_WINDOWS)
POOL_HALO = 16
N_EXPERTS = 8
TOP_K = 2
EPS = 1e-5
NEG_INF = -1e30
LOG2E = 1.4426950408889634
STATS_ROWS = 8
SKIP_LOG2 = 152.0
NORM_SLACK = 1.02

N_PIECES = 3

TM_TOKENS = 512
TQ_ATTN = 1024
TK_ATTN = 512
TM_EXPERT = 1024
FF_CHUNK = 512
FF_SUB = 256

VMEM_LIMIT_BYTES = 56 * 1024 * 1024


def _params(*sem, flags=None):
    return pltpu.CompilerParams(dimension_semantics=sem, vmem_limit_bytes=VMEM_LIMIT_BYTES, flags=flags)


def _const_spec(shape):
    zeros = (0,) * len(shape)
    return pl.BlockSpec(shape, lambda *_: zeros, pipeline_mode=pl.Buffered(1))


def _rms(v):
    return v * lax.rsqrt(jnp.mean(v * v, axis=-1, keepdims=True) + EPS)


def _split3(v):
    hi = v.astype(BF16)
    r1 = v - hi.astype(F32)
    mid = r1.astype(BF16)
    lo = (r1 - mid.astype(F32)).astype(BF16)
    return hi, mid, lo


def _lower_tri(n, strict):
    r = lax.broadcasted_iota(jnp.int32, (n, n), 0)
    c = lax.broadcasted_iota(jnp.int32, (n, n), 1)
    return jnp.where((c < r) if strict else (c <= r), 1.0, 0.0).astype(BF16)


def _layer0_kernel(x_ref, xp_ref, gmix_ref, gffn_ref, pw_ref, ps_ref, wg_ref, wu_ref, wd_ref,
                   o_ref, *, tiles_per_seq):
    tm = x_ref.shape[0]
    t = pl.program_id(0) % tiles_per_seq
    x = x_ref[...]
    gmix = gmix_ref[...]
    h = _rms(x) * gmix
    hp = jnp.where(t == 0, 0.0, _rms(xp_ref[...]) * gmix)
    ext = jnp.concatenate([hp, h], axis=0)
    pos = t * tm + lax.broadcasted_iota(jnp.int32, (tm, 1), 0)
    mixed = []
    for g, w in enumerate(POOL_WINDOWS):
        lanes = slice(g * POOL_GROUP, (g + 1) * POOL_GROUP)
        win = ext[:, lanes]
        shift = 1
        while shift < w:
            win = win + pltpu.roll(win, shift, axis=0)
            shift *= 2
        cnt = jnp.minimum(pos + 1, w).astype(F32)
        pooled = win[POOL_HALO:, :] / cnt - h[:, lanes]
        mixed.append(jnp.dot(pooled.astype(BF16), pw_ref[g], preferred_element_type=F32))
    x1 = x + jnp.concatenate(mixed, axis=1) * ps_ref[...]
    h2 = (_rms(x1) * gffn_ref[...]).astype(BF16)
    gate = jnp.dot(h2, wg_ref[...], preferred_element_type=F32)
    up = jnp.dot(h2, wu_ref[...], preferred_element_type=F32)
    act = (gate * jax.nn.sigmoid(gate) * up).astype(BF16)
    o_ref[...] = x1 + jnp.dot(act, wd_ref[...], preferred_element_type=F32)


def _layer0(x2d, seq, gmix, gffn, pool_w, pool_scale, wg, wu, wd):
    n_tok, d = x2d.shape
    tm = TM_TOKENS
    tiles_per_seq = seq // tm
    halo_blocks = tm // POOL_HALO
    d_ff = wg.shape[1]
    return pl.pallas_call(
        functools.partial(_layer0_kernel, tiles_per_seq=tiles_per_seq),
        grid=(n_tok // tm,),
        in_specs=[
            pl.BlockSpec((tm, d), lambda i: (i, 0)),
            pl.BlockSpec((POOL_HALO, d), lambda i: (jnp.maximum(i * halo_blocks - 1, 0), 0)),
            _const_spec((1, d)), _const_spec((1, d)),
            _const_spec(pool_w.shape), _const_spec((1, d)),
            _const_spec((d, d_ff)), _const_spec((d, d_ff)), _const_spec((d_ff, d)),
        ],
        out_specs=pl.BlockSpec((tm, d), lambda i: (i, 0)),
        out_shape=jax.ShapeDtypeStruct((n_tok, d), F32),
        compiler_params=_params("arbitrary"),
        name="layer0",
    )(x2d, x2d, gmix, gffn, pool_w, pool_scale, wg, wu, wd)


def _qkv_kernel(x_ref, gq_ref, gkv_ref, wq_ref, wk_ref, wv_ref, wf_ref, bf_ref, pq_ref, pk_ref,
                cq_ref, ck_ref, cv_ref, hsum_ref, q_ref, k_ref, v_ref, stats_ref, carry_ref, *, tiles_per_seq):
    tm = x_ref.shape[0]

    @pl.when(pl.program_id(0) % tiles_per_seq == 0)
    def _():
        carry_ref[...] = jnp.zeros_like(carry_ref)

    xn = _rms(x_ref[...])
    hq = (xn * gq_ref[...]).astype(BF16)
    hkv = (xn * gkv_ref[...]).astype(BF16)

    z = jnp.dot(hkv, wf_ref[...], preferred_element_type=F32) + bf_ref[...]
    logf = jnp.minimum(z, 0.0) - jnp.log1p(jnp.exp(-jnp.abs(z)))
    tri = _lower_tri(tm, strict=False)
    fcum = carry_ref[...] + sum(jnp.dot(tri, p, preferred_element_type=F32) for p in _split3(logf))
    carry_ref[...] = fcum[tm - 1:tm, :]
    hi, mid, lo = _split3(fcum * LOG2E)
    lane = lax.broadcasted_iota(jnp.int32, fcum.shape, 1)
    pieces = jnp.where(lane < N_HEADS, hi, jnp.where(lane < 2 * N_HEADS, mid, lo))

    q = jnp.dot(hq, wq_ref[...], preferred_element_type=F32) * (HEAD_DIM ** -0.5 * LOG2E)
    k = jnp.dot(hkv, wk_ref[...], preferred_element_type=F32)
    v = jnp.dot(hkv, wv_ref[...], preferred_element_type=F32)
    gq = jnp.dot(pieces, pq_ref[...], preferred_element_type=F32) + cq_ref[...]
    gk = jnp.dot(pieces, pk_ref[...], preferred_element_type=F32) + ck_ref[...]
    low = lax.broadcasted_iota(jnp.int32, (tm, HEAD_LANES), 1) < HEAD_DIM
    for pair in range(N_HEADS // 2):
        data = slice(pair * HEAD_LANES, (pair + 1) * HEAD_LANES)
        for odd in range(2):
            h = 2 * pair + odd
            keep = jnp.logical_not(low) if odd else low
            q_ref[0, h] = jnp.where(keep, q[:, data], gq[:, data]).astype(BF16)
            k_ref[0, h] = jnp.where(keep, k[:, data], gk[:, data]).astype(BF16)
            v_ref[0, h] = jnp.where(keep, v[:, data], cv_ref[:, data]).astype(BF16)

    ksq = jnp.dot((k * k).astype(BF16), hsum_ref[...], preferred_element_type=F32)
    qsq = jnp.dot((q * q).astype(BF16), hsum_ref[...], preferred_element_type=F32)
    f2 = fcum * LOG2E
    rows = [jnp.max(ksq, axis=0, keepdims=True), jnp.max(qsq, axis=0, keepdims=True),
            f2[tm - 1:tm, :], f2[0:1, :]]
    stats_ref[0] = jnp.concatenate(rows + [jnp.zeros((STATS_ROWS - len(rows), HEAD_LANES), F32)], axis=0)


def _qkv(x2d, batch, seq, gq, gkv, wq, wk, wv, wf, bf, pq, pk, cq, ck, cv, hsum):
    n_tok, d = x2d.shape
    tm = TM_TOKENS
    tiles_per_seq = seq // tm
    wide = d
    head_spec = pl.BlockSpec((1, N_HEADS, tm, HEAD_LANES),
                             lambda i: (i // tiles_per_seq, 0, i % tiles_per_seq, 0))
    head_shape = jax.ShapeDtypeStruct((batch, N_HEADS, seq, HEAD_LANES), BF16)
    return pl.pallas_call(
        functools.partial(_qkv_kernel, tiles_per_seq=tiles_per_seq),
        grid=(n_tok // tm,),
        in_specs=[
            pl.BlockSpec((tm, d), lambda i: (i, 0)),
            _const_spec((1, d)), _const_spec((1, d)),
            _const_spec((d, d)), _const_spec((d, d)), _const_spec((d, d)),
            _const_spec((d, HEAD_LANES)), _const_spec((1, HEAD_LANES)),
            _const_spec((HEAD_LANES, wide)), _const_spec((HEAD_LANES, wide)),
            _const_spec((1, wide)), _const_spec((1, wide)), _const_spec((1, wide)),
            _const_spec((d, HEAD_LANES)),
        ],
        out_specs=[head_spec, head_spec, head_spec,
                   pl.BlockSpec((1, STATS_ROWS, HEAD_LANES), lambda i: (i, 0, 0))],
        out_shape=[head_shape, head_shape, head_shape,
                   jax.ShapeDtypeStruct((n_tok // tm, STATS_ROWS, HEAD_LANES), F32)],
        scratch_shapes=[pltpu.VMEM((1, HEAD_LANES), F32)],
        compiler_params=_params("arbitrary"),
        name="qkv",
    )(x2d, gq, gkv, wq, wk, wv, wf, bf, pq, pk, cq, ck, cv, hsum)


def _attn_kernel(q_ref, k_ref, v_ref, kstat_ref, qstat_ref, o_ref, p_buf, corr_buf, m_buf, acc_buf, *, tk):
    tq = q_ref.shape[2]
    qi = pl.program_id(2)
    chunks_per_q = tq // tk
    assert chunks_per_q == 2
    n_below = qi * chunks_per_q
    heads = range(2)
    tri = (lax.broadcasted_iota(jnp.int32, (tk, tk), 0) >= lax.broadcasted_iota(jnp.int32, (tk, tk), 1))
    lane = lax.broadcasted_iota(jnp.int32, (1, HEAD_LANES), 1)

    def scores(hh, j, rows=slice(None)):
        k = k_ref[0, hh, pl.ds(pl.multiple_of(j * tk, tk), tk), :]
        return lax.dot_general(q_ref[0, hh, rows, :], k, (((1,), (1,)), ((), ())),
                               preferred_element_type=F32)

    def mask_top(s):
        masked = jnp.where(tri, s[:tk], NEG_INF)
        return masked if s.shape[0] == tk else jnp.concatenate([masked, s[tk:]], axis=0)

    def softmax_chunk(s, m):
        blocks = [s[:, c:c + HEAD_LANES] for c in range(0, s.shape[1], HEAD_LANES)]
        m_new = jnp.maximum(m, jnp.max(functools.reduce(jnp.maximum, blocks), axis=-1, keepdims=True))
        p = jnp.concatenate([jnp.exp2(b - m_new) for b in blocks], axis=1)
        return m_new, p.astype(BF16), jnp.exp2(m - m_new)

    def value_matmul(hh, j, p):
        v = v_ref[0, hh, pl.ds(pl.multiple_of(j * tk, tk), tk), :]
        return jnp.dot(p, v, preferred_element_type=F32)

    def prepare(hh, j, slot, valid=None):
        m_old = m_buf[hh]
        m, p, corr = softmax_chunk(scores(hh, j), m_old)
        m_buf[hh] = m if valid is None else jnp.where(valid, m, m_old)
        p_buf[hh, slot] = p
        corr_buf[hh, slot] = corr

    def consume(hh, j, slot):
        acc_buf[hh] = acc_buf[hh] * corr_buf[hh, slot] + value_matmul(hh, j, p_buf[hh, slot])

    first_needed = []
    for hh in heads:
        m, p, _ = softmax_chunk(mask_top(scores(hh, n_below)), jnp.full((tq, HEAD_LANES), NEG_INF, F32))
        acc = value_matmul(hh, n_below, p)
        m_low, p, corr = softmax_chunk(mask_top(scores(hh, n_below + 1, rows=slice(tk, tq))), m[tk:])
        m = jnp.concatenate([m[:tk], m_low], axis=0)
        m_buf[hh] = m
        acc_buf[hh] = jnp.concatenate([acc[:tk], acc[tk:] * corr + value_matmul(hh, n_below + 1, p)], axis=0)

        kst = kstat_ref[0, hh]
        qst = qstat_ref[0, hh]
        mine = lane == qi
        qn = jnp.sum(jnp.where(mine, qst[0:1], 0.0), axis=1, keepdims=True)
        f0 = jnp.sum(jnp.where(mine, qst[1:2], 0.0), axis=1, keepdims=True)
        bound = qn * kst[0:1] * NORM_SLACK + (f0 - kst[1:2])
        floor = jnp.min(m, axis=0, keepdims=True) - SKIP_LOG2
        dead = (bound < floor) & (lane < n_below)
        first_needed.append(jnp.sum(dead.astype(jnp.int32)))

    starts = [f // 2 * 2 for f in first_needed]
    joint_start = jnp.maximum(starts[0], starts[1])

    for hh in heads:
        prepare(hh, jnp.maximum(n_below - 1, 0), 0, valid=qi > 0)

    def pair_body(which, top):
        def body(jj, _):
            newest = top - 1 - 2 * jj
            for slot in range(2):
                for hh in which:
                    prepare(hh, jnp.maximum(newest - slot - 1, 0), 1 - slot)
                    consume(hh, newest - slot, slot)
            return 0
        return body

    lax.fori_loop(0, (n_below - joint_start) // 2, pair_body(heads, n_below), 0)
    for hh in heads:
        lax.fori_loop(0, (joint_start - starts[hh]) // 2, pair_body((hh,), joint_start), 0)
    outs = [acc_buf[hh] for hh in heads]
    even = outs[0] / outs[0][:, HEAD_DIM:HEAD_DIM + 1]
    odd = outs[1] / outs[1][:, 0:1]
    lane = lax.broadcasted_iota(jnp.int32, even.shape, 1)
    o_ref[0] = jnp.where(lane < HEAD_DIM, even, odd).astype(o_ref.dtype)


def _attention(q, k, v, kstat, qstat):
    batch, _, seq, _ = q.shape
    tq, tk = TQ_ATTN, TK_ATTN
    pair_kv = pl.BlockSpec((1, 2, seq, HEAD_LANES), lambda b, hp, i: (b, hp, 0, 0))
    pair_stat = pl.BlockSpec((1, 2, STATS_ROWS, HEAD_LANES), lambda b, hp, i: (b, hp, 0, 0))
    return pl.pallas_call(
        functools.partial(_attn_kernel, tk=tk),
        grid=(batch, N_HEADS // 2, seq // tq),
        in_specs=[pl.BlockSpec((1, 2, tq, HEAD_LANES), lambda b, hp, i: (b, hp, i, 0)), pair_kv, pair_kv,
                  pair_stat, pair_stat],
        out_specs=pl.BlockSpec((1, tq, HEAD_LANES), lambda b, hp, i: (b, i, hp)),
        out_shape=jax.ShapeDtypeStruct((batch, seq, D_MODEL), BF16),
        scratch_shapes=[pltpu.VMEM((2, 2, tq, tk), BF16),
                        pltpu.VMEM((2, 2, tq, HEAD_LANES), F32),
                        pltpu.VMEM((2, tq, HEAD_LANES), F32),
                        pltpu.VMEM((2, tq, HEAD_LANES), F32)],
        compiler_params=_params("arbitrary", "arbitrary", "arbitrary"),
        name="attn",
    )(q, k, v, kstat, qstat)


META_LANES = 8


def _oproj_kernel(o_ref, x_ref, wo_ref, g_ref, wrh_ref, wrl_ref,
                  x3_ref, h3_ref, meta_ref, meta_t_ref, cnt_ref, carry_ref):
    tm = x_ref.shape[0]

    @pl.when(pl.program_id(0) == 0)
    def _():
        carry_ref[...] = jnp.zeros_like(carry_ref)

    x3 = x_ref[...] + jnp.dot(o_ref[...], wo_ref[...], preferred_element_type=F32)
    x3_ref[...] = x3
    h3 = _rms(x3) * g_ref[...]
    h_hi = h3.astype(BF16)
    h_lo = (h3 - h_hi.astype(F32)).astype(BF16)
    h3_ref[...] = h_hi
    nt = (((1,), (1,)), ((), ()))
    logits = (lax.dot_general(wrh_ref[...], h_hi, nt, preferred_element_type=F32)
              + lax.dot_general(wrh_ref[...], h_lo, nt, preferred_element_type=F32)
              + lax.dot_general(wrl_ref[...], h_hi, nt, preferred_element_type=F32))
    lg = logits[:N_EXPERTS]
    expert = lax.broadcasted_iota(jnp.int32, lg.shape, 0)
    m1 = jnp.max(lg, axis=0, keepdims=True)
    i1 = jnp.min(jnp.where(lg == m1, expert, N_EXPERTS), axis=0, keepdims=True)
    lg2 = jnp.where(expert == i1, -jnp.inf, lg)
    m2 = jnp.max(lg2, axis=0, keepdims=True)
    i2 = jnp.min(jnp.where(lg2 == m2, expert, N_EXPERTS), axis=0, keepdims=True)
    e2 = jnp.exp(m2 - m1)
    w1 = 1.0 / (1.0 + e2)
    w2 = e2 / (1.0 + e2)
    sel1 = expert == i1
    sel2 = expert == i2
    chosen = jnp.where(sel1 | sel2, 1.0, 0.0)
    chosen16 = jnp.concatenate([chosen, jnp.zeros_like(chosen)], axis=0).astype(BF16)
    row_t = lax.broadcasted_iota(jnp.int32, (tm, tm), 0)
    col_t = lax.broadcasted_iota(jnp.int32, (tm, tm), 1)
    earlier = jnp.where(row_t < col_t, 1.0, 0.0).astype(BF16)
    before = carry_ref[...][:, 0:1] + jnp.dot(chosen16, earlier, preferred_element_type=F32)[:N_EXPERTS]
    carry_ref[...] = carry_ref[...] + jnp.sum(chosen, axis=1, keepdims=True)
    cnt_ref[...] = carry_ref[...]
    r1 = jnp.sum(jnp.where(sel1, before, 0.0), axis=0, keepdims=True)
    r2 = jnp.sum(jnp.where(sel2, before, 0.0), axis=0, keepdims=True)
    meta_t = jnp.concatenate([i1.astype(F32), i2.astype(F32), r1, r2, w1, w2,
                              jnp.zeros((HEAD_LANES - 6, tm), F32)], axis=0)
    meta_ref[...] = meta_t.T[:, :META_LANES]
    meta_t_ref[...] = meta_t[:META_LANES]


def _oproj(o2d, x2d, wo, g, wr_hi, wr_lo):
    n_tok, d = x2d.shape
    tm = TM_TOKENS
    row = lambda i: (i, 0)
    return pl.pallas_call(
        _oproj_kernel,
        grid=(n_tok // tm,),
        in_specs=[
            pl.BlockSpec((tm, d), row), pl.BlockSpec((tm, d), row),
            _const_spec((d, d)), _const_spec((1, d)),
            _const_spec(wr_hi.shape), _const_spec(wr_lo.shape),
        ],
        out_specs=[pl.BlockSpec((tm, d), row), pl.BlockSpec((tm, d), row),
                   pl.BlockSpec((tm, META_LANES), row),
                   pl.BlockSpec((META_LANES, tm), lambda i: (0, i)),
                   pl.BlockSpec((N_EXPERTS, HEAD_LANES), lambda i: (0, 0))],
        out_shape=[jax.ShapeDtypeStruct((n_tok, d), F32), jax.ShapeDtypeStruct((n_tok, d), BF16),
                   jax.ShapeDtypeStruct((n_tok, META_LANES), F32),
                   jax.ShapeDtypeStruct((META_LANES, n_tok), F32),
                   jax.ShapeDtypeStruct((N_EXPERTS, HEAD_LANES), F32)],
        scratch_shapes=[pltpu.VMEM((N_EXPERTS, HEAD_LANES), F32)],
        compiler_params=_params("arbitrary"),
        name="oproj",
    )(o2d, x2d, wo, g, wr_hi, wr_lo)


def _moe_kernel(te_ref, nu_ref, h_ref, wg_ref, wu_ref, wd_ref, y_ref):
    @pl.when(pl.program_id(1) == 0)
    def _():
        y_ref[...] = jnp.zeros_like(y_ref)

    @pl.when(pl.program_id(0) < nu_ref[0])
    def _():
        h = h_ref[...]
        for s0 in range(0, wg_ref.shape[2], FF_SUB):
            sub = slice(s0, s0 + FF_SUB)
            gate = jnp.dot(h, wg_ref[0, :, sub].astype(BF16), preferred_element_type=F32)
            up = jnp.dot(h, wu_ref[0, :, sub].astype(BF16), preferred_element_type=F32)
            act = (gate * jax.nn.sigmoid(gate) * up).astype(BF16)
            y_ref[...] += jnp.dot(act, wd_ref[0, sub, :].astype(BF16), preferred_element_type=F32)


def _moe(tile_expert, n_used, h_sorted, wg, wu, wd):
    n_slots, d = h_sorted.shape
    tm = TM_EXPERT
    fc = FF_CHUNK
    n_chunks = wg.shape[2] // fc
    def row_map(i, c, te, nu):
        return (jnp.minimum(i, nu[0] - 1), 0)
    def chunk(i, c, te, nu):
        return jnp.where(i < nu[0], c, n_chunks - 1)
    return pl.pallas_call(
        _moe_kernel,
        grid_spec=pltpu.PrefetchScalarGridSpec(
            num_scalar_prefetch=2,
            grid=(n_slots // tm, n_chunks),
            in_specs=[
                pl.BlockSpec((tm, d), row_map),
                pl.BlockSpec((1, d, fc), lambda i, c, te, nu: (te[i], 0, chunk(i, c, te, nu))),
                pl.BlockSpec((1, d, fc), lambda i, c, te, nu: (te[i], 0, chunk(i, c, te, nu))),
                pl.BlockSpec((1, fc, d), lambda i, c, te, nu: (te[i], chunk(i, c, te, nu), 0)),
            ],
            out_specs=pl.BlockSpec((tm, d), lambda i, c, te, nu: (i, 0)),
        ),
        out_shape=jax.ShapeDtypeStruct((n_slots, d), F32),
        compiler_params=_params("arbitrary", "arbitrary"),
        name="moe",
    )(tile_expert, n_used, h_sorted, wg, wu, wd)


def _final_kernel(x_ref, ya_ref, yb_ref, meta_ref, g_ref, o_ref):
    meta = meta_ref[...]
    x = x_ref[...] + meta[:, 4:5] * ya_ref[...] + meta[:, 5:6] * yb_ref[...]
    o_ref[...] = _rms(x) * g_ref[...]


def _final(x2d, ya, yb, meta, g):
    n_tok, d = x2d.shape
    tm = TM_TOKENS
    row = lambda i: (i, 0)
    big = pl.BlockSpec((tm, d), row)
    return pl.pallas_call(
        _final_kernel,
        grid=(n_tok // tm,),
        in_specs=[big, big, big, pl.BlockSpec((tm, META_LANES), row), _const_spec((1, d))],
        out_specs=big,
        out_shape=jax.ShapeDtypeStruct((n_tok, d), F32),
        compiler_params=_params("arbitrary"),
        name="final",
    )(x2d, ya, yb, meta, g)


def _gate_constants():
    wide = D_MODEL
    pq = np.zeros((HEAD_LANES, wide), np.float32)
    pk = np.zeros((HEAD_LANES, wide), np.float32)
    cq = np.zeros((1, wide), np.float32)
    ck = np.zeros((1, wide), np.float32)
    cv = np.zeros((1, wide), np.float32)
    for h in range(N_HEADS):
        base = (h // 2) * HEAD_LANES + (HEAD_DIM if h % 2 == 0 else 0)
        cv[0, base] = 1.0
        for piece in range(N_PIECES):
            r = piece * N_HEADS + h
            pq[r, base + piece] = 1.0
            ck[0, base + piece] = 1.0
            cq[0, base + N_PIECES + piece] = 1.0
            pk[r, base + N_PIECES + piece] = -1.0
    hsum = np.zeros((D_MODEL, HEAD_LANES), np.float32)
    hsum[np.arange(D_MODEL), np.arange(D_MODEL) // HEAD_DIM] = 1.0
    return (jnp.asarray(pq, BF16), jnp.asarray(pk, BF16), jnp.asarray(cq), jnp.asarray(ck), jnp.asarray(cv),
            jnp.asarray(hsum, BF16))


def _skip_tables(stats, batch, seq):
    tiles = seq // TM_TOKENS
    per_q = TQ_ATTN // TM_TOKENS
    st = stats.reshape(batch, tiles, STATS_ROWS, HEAD_LANES)[..., :N_HEADS]
    st = jnp.swapaxes(st, 1, 3)
    knorm = lax.cummax(jnp.sqrt(st[:, :, 0]), axis=2)
    qnorm = jnp.sqrt(jnp.max(st[:, :, 1].reshape(batch, N_HEADS, tiles // per_q, per_q), axis=-1))
    f_start = st[:, :, 3].reshape(batch, N_HEADS, tiles // per_q, per_q)[..., 0]

    def table(row0, row1):
        t = jnp.stack([row0, row1], axis=2)
        return jnp.pad(t, ((0, 0), (0, 0), (0, STATS_ROWS - 2), (0, HEAD_LANES - t.shape[-1])))

    return table(knorm, st[:, :, 2]), table(qnorm, f_start)


def kernel(x, norm_mix, norm_ffn, pool_w, pool_scale, norm_kv, w_kv, b_f, w_q, w_o, ffn_gate, ffn_up,
           ffn_down, w_router, exp_gate, exp_up, exp_down, norm_final):
    batch, seq, d = x.shape
    n_tok = batch * seq
    assert d == D_MODEL and seq % TQ_ATTN == 0 and TQ_ATTN == 2 * TK_ATTN and TK_ATTN == TM_TOKENS
    assert seq // TK_ATTN <= HEAD_LANES
    row = lambda v: v.reshape(1, -1).astype(F32)
    x2d = x.reshape(n_tok, d)

    x2 = _layer0(x2d, seq, row(norm_mix[0]), row(norm_ffn[0]), pool_w[0].astype(BF16), row(pool_scale[0]),
                 ffn_gate[0].astype(BF16), ffn_up[0].astype(BF16), ffn_down[0].astype(BF16))

    wq = w_q[0].astype(BF16)
    wk = w_kv[:, :d].astype(BF16)
    wv = w_kv[:, d:2 * d].astype(BF16)
    wf = jnp.zeros((d, HEAD_LANES), F32).at[:, :N_PIECES * N_HEADS].set(
        jnp.tile(w_kv[:, 2 * d:], (1, N_PIECES))).astype(BF16)
    bf = jnp.zeros((1, HEAD_LANES), F32).at[0, :N_PIECES * N_HEADS].set(jnp.tile(b_f.astype(F32), N_PIECES))
    q, k, v, stats = _qkv(x2, batch, seq, row(norm_mix[1]), row(norm_kv), wq, wk, wv, wf, bf,
                          *_gate_constants())

    o = _attention(q, k, v, *_skip_tables(stats, batch, seq))

    wr = jnp.zeros((2 * N_EXPERTS, d), F32).at[:N_EXPERTS].set(w_router[0].T)
    wr_hi = wr.astype(BF16)
    wr_lo = (wr - wr_hi.astype(F32)).astype(BF16)
    x3, h3, meta, meta_t, counts = _oproj(o.reshape(n_tok, d), x2, w_o[0].astype(BF16), row(norm_ffn[1]), wr_hi, wr_lo)

    tm = TM_EXPERT
    n_tiles = n_tok * TOP_K // tm + N_EXPERTS
    counts = counts[:, 0].astype(jnp.int32)
    padded = (counts + tm - 1) // tm * tm
    group_end = jnp.cumsum(padded)
    group_start = group_end - padded
    expert = meta_t[0:2].astype(jnp.int32)
    slot = group_start[expert] + meta_t[2:4].astype(jnp.int32)
    n_used = (group_end[-1] // tm).astype(jnp.int32).reshape(1)
    tile_expert = jnp.sum(jnp.arange(n_tiles)[:, None] * tm >= group_end[None, :], axis=1)
    last_expert = jnp.max(jnp.where(counts > 0, jnp.arange(N_EXPERTS), 0))
    tile_expert = jnp.minimum(tile_expert, last_expert).astype(jnp.int32)
    order = jnp.argsort(slot.reshape(-1)).astype(jnp.int32)
    tile_ids = jnp.arange(n_tiles, dtype=jnp.int32)
    within = jnp.arange(tm, dtype=jnp.int32)[None, :]
    rank = (tile_ids * tm - group_start[tile_expert])[:, None] + within
    compact = (jnp.cumsum(counts) - counts)[tile_expert][:, None] + rank
    src = jnp.where(rank < counts[tile_expert][:, None],
                    order[jnp.minimum(compact, n_tok * TOP_K - 1)] % n_tok,
                    (tile_ids[:, None] * tm + within) % n_tok).reshape(-1)
    h_sorted = jnp.take(h3, src, axis=0, mode="clip")

    y = _moe(tile_expert, n_used, h_sorted, exp_gate[0], exp_up[0], exp_down[0])

    ya = jnp.take(y, slot[0], axis=0, mode="clip")
    yb = jnp.take(y, slot[1], axis=0, mode="clip")
    out = _final(x3, ya, yb, meta, row(norm_final))
    return out.reshape(batch, seq, d)
```

```python
import functools

import jax
import jax.numpy as jnp
import numpy as np
from jax import lax
from jax.experimental import pallas as pl
from jax.experimental.pallas import tpu as pltpu

F32 = jnp.float32
BF16 = jnp.bfloat16

D_MODEL = 1024
HEAD_DIM = 64
N_HEADS = D_MODEL // HEAD_DIM
HEAD_LANES = 128
POOL_WINDOWS = (2, 4, 8, 16)
POOL_GROUP = D_MODEL // len(POOL_WINDOWS)
POOL_HALO = 16
N_EXPERTS = 8
TOP_K = 2
EPS = 1e-5
NEG_INF = -1e30
LOG2E = 1.4426950408889634
STATS_ROWS = 8
SKIP_LOG2 = 152.0
NORM_SLACK = 1.02

N_PIECES = 3

TM_TOKENS = 512
TQ_ATTN = 1024
TK_ATTN = 512
TM_EXPERT = 1024
FF_CHUNK = 512
FF_SUB = 256

VMEM_LIMIT_BYTES = 56 * 1024 * 1024


def _params(*sem, flags=None):
    return pltpu.CompilerParams(dimension_semantics=sem, vmem_limit_bytes=VMEM_LIMIT_BYTES, flags=flags)


def _const_spec(shape):
    zeros = (0,) * len(shape)
    return pl.BlockSpec(shape, lambda *_: zeros, pipeline_mode=pl.Buffered(1))


def _rms(v):
    return v * lax.rsqrt(jnp.mean(v * v, axis=-1, keepdims=True) + EPS)


def _split3(v):
    hi = v.astype(BF16)
    r1 = v - hi.astype(F32)
    mid = r1.astype(BF16)
    lo = (r1 - mid.astype(F32)).astype(BF16)
    return hi, mid, lo


def _lower_tri(n, strict):
    r = lax.broadcasted_iota(jnp.int32, (n, n), 0)
    c = lax.broadcasted_iota(jnp.int32, (n, n), 1)
    return jnp.where((c < r) if strict else (c <= r), 1.0, 0.0).astype(BF16)


def _layer0_kernel(x_ref, xp_ref, gmix_ref, gffn_ref, pw_ref, ps_ref, wg_ref, wu_ref, wd_ref,
                   o_ref, *, tiles_per_seq):
    tm = x_ref.shape[0]
    t = pl.program_id(0) % tiles_per_seq
    x = x_ref[...]
    gmix = gmix_ref[...]
    h = _rms(x) * gmix
    hp = jnp.where(t == 0, 0.0, _rms(xp_ref[...]) * gmix)
    ext = jnp.concatenate([hp, h], axis=0)
    pos = t * tm + lax.broadcasted_iota(jnp.int32, (tm, 1), 0)
    mixed = []
    for g, w in enumerate(POOL_WINDOWS):
        lanes = slice(g * POOL_GROUP, (g + 1) * POOL_GROUP)
        win = ext[:, lanes]
        shift = 1
        while shift < w:
            win = win + pltpu.roll(win, shift, axis=0)
            shift *= 2
        cnt = jnp.minimum(pos + 1, w).astype(F32)
        pooled = win[POOL_HALO:, :] / cnt - h[:, lanes]
        mixed.append(jnp.dot(pooled.astype(BF16), pw_ref[g], preferred_element_type=F32))
    x1 = x + jnp.concatenate(mixed, axis=1) * ps_ref[...]
    h2 = (_rms(x1) * gffn_ref[...]).astype(BF16)
    gate = jnp.dot(h2, wg_ref[...], preferred_element_type=F32)
    up = jnp.dot(h2, wu_ref[...], preferred_element_type=F32)
    act = (gate * jax.nn.sigmoid(gate) * up).astype(BF16)
    o_ref[...] = x1 + jnp.dot(act, wd_ref[...], preferred_element_type=F32)


def _layer0(x2d, seq, gmix, gffn, pool_w, pool_scale, wg, wu, wd):
    n_tok, d = x2d.shape
    tm = TM_TOKENS
    tiles_per_seq = seq // tm
    halo_blocks = tm // POOL_HALO
    d_ff = wg.shape[1]
    return pl.pallas_call(
        functools.partial(_layer0_kernel, tiles_per_seq=tiles_per_seq),
        grid=(n_tok // tm,),
        in_specs=[
            pl.BlockSpec((tm, d), lambda i: (i, 0)),
            pl.BlockSpec((POOL_HALO, d), lambda i: (jnp.maximum(i * halo_blocks - 1, 0), 0)),
            _const_spec((1, d)), _const_spec((1, d)),
            _const_spec(pool_w.shape), _const_spec((1, d)),
            _const_spec((d, d_ff)), _const_spec((d, d_ff)), _const_spec((d_ff, d)),
        ],
        out_specs=pl.BlockSpec((tm, d), lambda i: (i, 0)),
        out_shape=jax.ShapeDtypeStruct((n_tok, d), F32),
        compiler_params=_params("arbitrary"),
        name="layer0",
    )(x2d, x2d, gmix, gffn, pool_w, pool_scale, wg, wu, wd)


def _qkv_kernel(x_ref, gq_ref, gkv_ref, wq_ref, wk_ref, wv_ref, wf_ref, bf_ref, pq_ref, pk_ref,
                cq_ref, ck_ref, cv_ref, hsum_ref, q_ref, k_ref, v_ref, stats_ref, carry_ref, *, tiles_per_seq):
    tm = x_ref.shape[0]

    @pl.when(pl.program_id(0) % tiles_per_seq == 0)
    def _():
        carry_ref[...] = jnp.zeros_like(carry_ref)

    xn = _rms(x_ref[...])
    hq = (xn * gq_ref[...]).astype(BF16)
    hkv = (xn * gkv_ref[...]).astype(BF16)

    z = jnp.dot(hkv, wf_ref[...], preferred_element_type=F32) + bf_ref[...]
    logf = jnp.minimum(z, 0.0) - jnp.log1p(jnp.exp(-jnp.abs(z)))
    tri = _lower_tri(tm, strict=False)
    fcum = carry_ref[...] + sum(jnp.dot(tri, p, preferred_element_type=F32) for p in _split3(logf))
    carry_ref[...] = fcum[tm - 1:tm, :]
    hi, mid, lo = _split3(fcum * LOG2E)
    lane = lax.broadcasted_iota(jnp.int32, fcum.shape, 1)
    pieces = jnp.where(lane < N_HEADS, hi, jnp.where(lane < 2 * N_HEADS, mid, lo))

    q = jnp.dot(hq, wq_ref[...], preferred_element_type=F32) * (HEAD_DIM ** -0.5 * LOG2E)
    k = jnp.dot(hkv, wk_ref[...], preferred_element_type=F32)
    v = jnp.dot(hkv, wv_ref[...], preferred_element_type=F32)
    gq = jnp.dot(pieces, pq_ref[...], preferred_element_type=F32) + cq_ref[...]
    gk = jnp.dot(pieces, pk_ref[...], preferred_element_type=F32) + ck_ref[...]
    low = lax.broadcasted_iota(jnp.int32, (tm, HEAD_LANES), 1) < HEAD_DIM
    for pair in range(N_HEADS // 2):
        data = slice(pair * HEAD_LANES, (pair + 1) * HEAD_LANES)
        for odd in range(2):
            h = 2 * pair + odd
            keep = jnp.logical_not(low) if odd else low
            q_ref[0, h] = jnp.where(keep, q[:, data], gq[:, data]).astype(BF16)
            k_ref[0, h] = jnp.where(keep, k[:, data], gk[:, data]).astype(BF16)
            v_ref[0, h] = jnp.where(keep, v[:, data], cv_ref[:, data]).astype(BF16)

    ksq = jnp.dot((k * k).astype(BF16), hsum_ref[...], preferred_element_type=F32)
    qsq = jnp.dot((q * q).astype(BF16), hsum_ref[...], preferred_element_type=F32)
    f2 = fcum * LOG2E
    rows = [jnp.max(ksq, axis=0, keepdims=True), jnp.max(qsq, axis=0, keepdims=True),
            f2[tm - 1:tm, :], f2[0:1, :]]
    stats_ref[0] = jnp.concatenate(rows + [jnp.zeros((STATS_ROWS - len(rows), HEAD_LANES), F32)], axis=0)


def _qkv(x2d, batch, seq, gq, gkv, wq, wk, wv, wf, bf, pq, pk, cq, ck, cv, hsum):
    n_tok, d = x2d.shape
    tm = TM_TOKENS
    tiles_per_seq = seq // tm
    wide = d
    head_spec = pl.BlockSpec((1, N_HEADS, tm, HEAD_LANES),
                             lambda i: (i // tiles_per_seq, 0, i % tiles_per_seq, 0))
    head_shape = jax.ShapeDtypeStruct((batch, N_HEADS, seq, HEAD_LANES), BF16)
    return pl.pallas_call(
        functools.partial(_qkv_kernel, tiles_per_seq=tiles_per_seq),
        grid=(n_tok // tm,),
        in_specs=[
            pl.BlockSpec((tm, d), lambda i: (i, 0)),
            _const_spec((1, d)), _const_spec((1, d)),
            _const_spec((d, d)), _const_spec((d, d)), _const_spec((d, d)),
            _const_spec((d, HEAD_LANES)), _const_spec((1, HEAD_LANES)),
            _const_spec((HEAD_LANES, wide)), _const_spec((HEAD_LANES, wide)),
            _const_spec((1, wide)), _const_spec((1, wide)), _const_spec((1, wide)),
            _const_spec((d, HEAD_LANES)),
        ],
        out_specs=[head_spec, head_spec, head_spec,
                   pl.BlockSpec((1, STATS_ROWS, HEAD_LANES), lambda i: (i, 0, 0))],
        out_shape=[head_shape, head_shape, head_shape,
                   jax.ShapeDtypeStruct((n_tok // tm, STATS_ROWS, HEAD_LANES), F32)],
        scratch_shapes=[pltpu.VMEM((1, HEAD_LANES), F32)],
        compiler_params=_params("arbitrary"),
        name="qkv",
    )(x2d, gq, gkv, wq, wk, wv, wf, bf, pq, pk, cq, ck, cv, hsum)


def _attn_kernel(q_ref, k_ref, v_ref, kstat_ref, qstat_ref, o_ref, p_buf, corr_buf, m_buf, acc_buf, *, tk):
    tq = q_ref.shape[2]
    qi = pl.program_id(2)
    chunks_per_q = tq // tk
    assert chunks_per_q == 2
    n_below = qi * chunks_per_q
    heads = range(2)
    tri = (lax.broadcasted_iota(jnp.int32, (tk, tk), 0) >= lax.broadcasted_iota(jnp.int32, (tk, tk), 1))
    lane = lax.broadcasted_iota(jnp.int32, (1, HEAD_LANES), 1)

    def scores(hh, j, rows=slice(None)):
        k = k_ref[0, hh, pl.ds(pl.multiple_of(j * tk, tk), tk), :]
        return lax.dot_general(q_ref[0, hh, rows, :], k, (((1,), (1,)), ((), ())),
                               preferred_element_type=F32)

    def mask_top(s):
        masked = jnp.where(tri, s[:tk], NEG_INF)
        return masked if s.shape[0] == tk else jnp.concatenate([masked, s[tk:]], axis=0)

    def softmax_chunk(s, m):
        blocks = [s[:, c:c + HEAD_LANES] for c in range(0, s.shape[1], HEAD_LANES)]
        m_new = jnp.maximum(m, jnp.max(functools.reduce(jnp.maximum, blocks), axis=-1, keepdims=True))
        p = jnp.concatenate([jnp.exp2(b - m_new) for b in blocks], axis=1)
        return m_new, p.astype(BF16), jnp.exp2(m - m_new)

    def value_matmul(hh, j, p):
        v = v_ref[0, hh, pl.ds(pl.multiple_of(j * tk, tk), tk), :]
        return jnp.dot(p, v, preferred_element_type=F32)

    def prepare(hh, j, slot, valid=None):
        m_old = m_buf[hh]
        m, p, corr = softmax_chunk(scores(hh, j), m_old)
        m_buf[hh] = m if valid is None else jnp.where(valid, m, m_old)
        p_buf[hh, slot] = p
        corr_buf[hh, slot] = corr

    def consume(hh, j, slot):
        acc_buf[hh] = acc_buf[hh] * corr_buf[hh, slot] + value_matmul(hh, j, p_buf[hh, slot])

    first_needed = []
    for hh in heads:
        m, p, _ = softmax_chunk(mask_top(scores(hh, n_below)), jnp.full((tq, HEAD_LANES), NEG_INF, F32))
        acc = value_matmul(hh, n_below, p)
        m_low, p, corr = softmax_chunk(mask_top(scores(hh, n_below + 1, rows=slice(tk, tq))), m[tk:])
        m = jnp.concatenate([m[:tk], m_low], axis=0)
        m_buf[hh] = m
        acc_buf[hh] = jnp.concatenate([acc[:tk], acc[tk:] * corr + value_matmul(hh, n_below + 1, p)], axis=0)

        kst = kstat_ref[0, hh]
        qst = qstat_ref[0, hh]
        mine = lane == qi
        qn = jnp.sum(jnp.where(mine, qst[0:1], 0.0), axis=1, keepdims=True)
        f0 = jnp.sum(jnp.where(mine, qst[1:2], 0.0), axis=1, keepdims=True)
        bound = qn * kst[0:1] * NORM_SLACK + (f0 - kst[1:2])
        floor = jnp.min(m, axis=0, keepdims=True) - SKIP_LOG2
        dead = (bound < floor) & (lane < n_below)
        first_needed.append(jnp.sum(dead.astype(jnp.int32)))

    starts = [f // 2 * 2 for f in first_needed]
    joint_start = jnp.maximum(starts[0], starts[1])

    for hh in heads:
        prepare(hh, jnp.maximum(n_below - 1, 0), 0, valid=qi > 0)

    def pair_body(which, top):
        def body(jj, _):
            newest = top - 1 - 2 * jj
            for slot in range(2):
                for hh in which:
                    prepare(hh, jnp.maximum(newest - slot - 1, 0), 1 - slot)
                    consume(hh, newest - slot, slot)
            return 0
        return body

    lax.fori_loop(0, (n_below - joint_start) // 2, pair_body(heads, n_below), 0)
    for hh in heads:
        lax.fori_loop(0, (joint_start - starts[hh]) // 2, pair_body((hh,), joint_start), 0)
    outs = [acc_buf[hh] for hh in heads]
    even = outs[0] / outs[0][:, HEAD_DIM:HEAD_DIM + 1]
    odd = outs[1] / outs[1][:, 0:1]
    lane = lax.broadcasted_iota(jnp.int32, even.shape, 1)
    o_ref[0] = jnp.where(lane < HEAD_DIM, even, odd).astype(o_ref.dtype)


def _attention(q, k, v, kstat, qstat):
    batch, _, seq, _ = q.shape
    tq, tk = TQ_ATTN, TK_ATTN
    pair_kv = pl.BlockSpec((1, 2, seq, HEAD_LANES), lambda b, hp, i: (b, hp, 0, 0))
    pair_stat = pl.BlockSpec((1, 2, STATS_ROWS, HEAD_LANES), lambda b, hp, i: (b, hp, 0, 0))
    return pl.pallas_call(
        functools.partial(_attn_kernel, tk=tk),
        grid=(batch, N_HEADS // 2, seq // tq),
        in_specs=[pl.BlockSpec((1, 2, tq, HEAD_LANES), lambda b, hp, i: (b, hp, i, 0)), pair_kv, pair_kv,
                  pair_stat, pair_stat],
        out_specs=pl.BlockSpec((1, tq, HEAD_LANES), lambda b, hp, i: (b, i, hp)),
        out_shape=jax.ShapeDtypeStruct((batch, seq, D_MODEL), BF16),
        scratch_shapes=[pltpu.VMEM((2, 2, tq, tk), BF16),
                        pltpu.VMEM((2, 2, tq, HEAD_LANES), F32),
                        pltpu.VMEM((2, tq, HEAD_LANES), F32),
                        pltpu.VMEM((2, tq, HEAD_LANES), F32)],
        compiler_params=_params("arbitrary", "arbitrary", "arbitrary"),
        name="attn",
    )(q, k, v, kstat, qstat)


META_LANES = 8


def _oproj_kernel(o_ref, x_ref, wo_ref, g_ref, wrh_ref, wrl_ref,
                  x3_ref, h3_ref, meta_ref, meta_t_ref, cnt_ref, carry_ref):
    tm = x_ref.shape[0]

    @pl.when(pl.program_id(0) == 0)
    def _():
        carry_ref[...] = jnp.zeros_like(carry_ref)

    x3 = x_ref[...] + jnp.dot(o_ref[...], wo_ref[...], preferred_element_type=F32)
    x3_ref[...] = x3
    h3 = _rms(x3) * g_ref[...]
    h_hi = h3.astype(BF16)
    h_lo = (h3 - h_hi.astype(F32)).astype(BF16)
    h3_ref[...] = h_hi
    nt = (((1,), (1,)), ((), ()))
    logits = (lax.dot_general(wrh_ref[...], h_hi, nt, preferred_element_type=F32)
              + lax.dot_general(wrh_ref[...], h_lo, nt, preferred_element_type=F32)
              + lax.dot_general(wrl_ref[...], h_hi, nt, preferred_element_type=F32))
    lg = logits[:N_EXPERTS]
    expert = lax.broadcasted_iota(jnp.int32, lg.shape, 0)
    m1 = jnp.max(lg, axis=0, keepdims=True)
    i1 = jnp.min(jnp.where(lg == m1, expert, N_EXPERTS), axis=0, keepdims=True)
    lg2 = jnp.where(expert == i1, -jnp.inf, lg)
    m2 = jnp.max(lg2, axis=0, keepdims=True)
    i2 = jnp.min(jnp.where(lg2 == m2, expert, N_EXPERTS), axis=0, keepdims=True)
    e2 = jnp.exp(m2 - m1)
    w1 = 1.0 / (1.0 + e2)
    w2 = e2 / (1.0 + e2)
    sel1 = expert == i1
    sel2 = expert == i2
    chosen = jnp.where(sel1 | sel2, 1.0, 0.0)
    chosen16 = jnp.concatenate([chosen, jnp.zeros_like(chosen)], axis=0).astype(BF16)
    row_t = lax.broadcasted_iota(jnp.int32, (tm, tm), 0)
    col_t = lax.broadcasted_iota(jnp.int32, (tm, tm), 1)
    earlier = jnp.where(row_t < col_t, 1.0, 0.0).astype(BF16)
    before = carry_ref[...][:, 0:1] + jnp.dot(chosen16, earlier, preferred_element_type=F32)[:N_EXPERTS]
    carry_ref[...] = carry_ref[...] + jnp.sum(chosen, axis=1, keepdims=True)
    cnt_ref[...] = carry_ref[...]
    r1 = jnp.sum(jnp.where(sel1, before, 0.0), axis=0, keepdims=True)
    r2 = jnp.sum(jnp.where(sel2, before, 0.0), axis=0, keepdims=True)
    meta_t = jnp.concatenate([i1.astype(F32), i2.astype(F32), r1, r2, w1, w2,
                              jnp.zeros((HEAD_LANES - 6, tm), F32)], axis=0)
    meta_ref[...] = meta_t.T[:, :META_LANES]
    meta_t_ref[...] = meta_t[:META_LANES]


def _oproj(o2d, x2d, wo, g, wr_hi, wr_lo):
    n_tok, d = x2d.shape
    tm = TM_TOKENS
    row = lambda i: (i, 0)
    return pl.pallas_call(
        _oproj_kernel,
        grid=(n_tok // tm,),
        in_specs=[
            pl.BlockSpec((tm, d), row), pl.BlockSpec((tm, d), row),
            _const_spec((d, d)), _const_spec((1, d)),
            _const_spec(wr_hi.shape), _const_spec(wr_lo.shape),
        ],
        out_specs=[pl.BlockSpec((tm, d), row), pl.BlockSpec((tm, d), row),
                   pl.BlockSpec((tm, META_LANES), row),
                   pl.BlockSpec((META_LANES, tm), lambda i: (0, i)),
                   pl.BlockSpec((N_EXPERTS, HEAD_LANES), lambda i: (0, 0))],
        out_shape=[jax.ShapeDtypeStruct((n_tok, d), F32), jax.ShapeDtypeStruct((n_tok, d), BF16),
                   jax.ShapeDtypeStruct((n_tok, META_LANES), F32),
                   jax.ShapeDtypeStruct((META_LANES, n_tok), F32),
                   jax.ShapeDtypeStruct((N_EXPERTS, HEAD_LANES), F32)],
        scratch_shapes=[pltpu.VMEM((N_EXPERTS, HEAD_LANES), F32)],
        compiler_params=_params("arbitrary"),
        name="oproj",
    )(o2d, x2d, wo, g, wr_hi, wr_lo)


def _moe_kernel(te_ref, nu_ref, h_ref, wg_ref, wu_ref, wd_ref, y_ref):
    @pl.when(pl.program_id(1) == 0)
    def _():
        y_ref[...] = jnp.zeros_like(y_ref)

    @pl.when(pl.program_id(0) < nu_ref[0])
    def _():
        h = h_ref[...]
        for s0 in range(0, wg_ref.shape[2], FF_SUB):
            sub = slice(s0, s0 + FF_SUB)
            gate = jnp.dot(h, wg_ref[0, :, sub].astype(BF16), preferred_element_type=F32)
            up = jnp.dot(h, wu_ref[0, :, sub].astype(BF16), preferred_element_type=F32)
            act = (gate * jax.nn.sigmoid(gate) * up).astype(BF16)
            y_ref[...] += jnp.dot(act, wd_ref[0, sub, :].astype(BF16), preferred_element_type=F32)


def _moe(tile_expert, n_used, h_sorted, wg, wu, wd):
    n_slots, d = h_sorted.shape
    tm = TM_EXPERT
    fc = FF_CHUNK
    n_chunks = wg.shape[2] // fc
    def row_map(i, c, te, nu):
        return (jnp.minimum(i, nu[0] - 1), 0)
    def chunk(i, c, te, nu):
        return jnp.where(i < nu[0], c, n_chunks - 1)
    return pl.pallas_call(
        _moe_kernel,
        grid_spec=pltpu.PrefetchScalarGridSpec(
            num_scalar_prefetch=2,
            grid=(n_slots // tm, n_chunks),
            in_specs=[
                pl.BlockSpec((tm, d), row_map),
                pl.BlockSpec((1, d, fc), lambda i, c, te, nu: (te[i], 0, chunk(i, c, te, nu))),
                pl.BlockSpec((1, d, fc), lambda i, c, te, nu: (te[i], 0, chunk(i, c, te, nu))),
                pl.BlockSpec((1, fc, d), lambda i, c, te, nu: (te[i], chunk(i, c, te, nu), 0)),
            ],
            out_specs=pl.BlockSpec((tm, d), lambda i, c, te, nu: (i, 0)),
        ),
        out_shape=jax.ShapeDtypeStruct((n_slots, d), F32),
        compiler_params=_params("arbitrary", "arbitrary"),
        name="moe",
    )(tile_expert, n_used, h_sorted, wg, wu, wd)


def _final_kernel(x_ref, ya_ref, yb_ref, meta_ref, g_ref, o_ref):
    meta = meta_ref[...]
    x = x_ref[...] + meta[:, 4:5] * ya_ref[...] + meta[:, 5:6] * yb_ref[...]
    o_ref[...] = _rms(x) * g_ref[...]


def _final(x2d, ya, yb, meta, g):
    n_tok, d = x2d.shape
    tm = TM_TOKENS
    row = lambda i: (i, 0)
    big = pl.BlockSpec((tm, d), row)
    return pl.pallas_call(
        _final_kernel,
        grid=(n_tok // tm,),
        in_specs=[big, big, big, pl.BlockSpec((tm, META_LANES), row), _const_spec((1, d))],
        out_specs=big,
        out_shape=jax.ShapeDtypeStruct((n_tok, d), F32),
        compiler_params=_params("arbitrary"),
        name="final",
    )(x2d, ya, yb, meta, g)


def _gate_constants():
    wide = D_MODEL
    pq = np.zeros((HEAD_LANES, wide), np.float32)
    pk = np.zeros((HEAD_LANES, wide), np.float32)
    cq = np.zeros((1, wide), np.float32)
    ck = np.zeros((1, wide), np.float32)
    cv = np.zeros((1, wide), np.float32)
    for h in range(N_HEADS):
        base = (h // 2) * HEAD_LANES + (HEAD_DIM if h % 2 == 0 else 0)
        cv[0, base] = 1.0
        for piece in range(N_PIECES):
            r = piece * N_HEADS + h
            pq[r, base + piece] = 1.0
            ck[0, base + piece] = 1.0
            cq[0, base + N_PIECES + piece] = 1.0
            pk[r, base + N_PIECES + piece] = -1.0
    hsum = np.zeros((D_MODEL, HEAD_LANES), np.float32)
    hsum[np.arange(D_MODEL), np.arange(D_MODEL) // HEAD_DIM] = 1.0
    return (jnp.asarray(pq, BF16), jnp.asarray(pk, BF16), jnp.asarray(cq), jnp.asarray(ck), jnp.asarray(cv),
            jnp.asarray(hsum, BF16))


def _skip_tables(stats, batch, seq):
    tiles = seq // TM_TOKENS
    per_q = TQ_ATTN // TM_TOKENS
    st = stats.reshape(batch, tiles, STATS_ROWS, HEAD_LANES)[..., :N_HEADS]
    st = jnp.swapaxes(st, 1, 3)
    knorm = lax.cummax(jnp.sqrt(st[:, :, 0]), axis=2)
    qnorm = jnp.sqrt(jnp.max(st[:, :, 1].reshape(batch, N_HEADS, tiles // per_q, per_q), axis=-1))
    f_start = st[:, :, 3].reshape(batch, N_HEADS, tiles // per_q, per_q)[..., 0]

    def table(row0, row1):
        t = jnp.stack([row0, row1], axis=2)
        return jnp.pad(t, ((0, 0), (0, 0), (0, STATS_ROWS - 2), (0, HEAD_LANES - t.shape[-1])))

    return table(knorm, st[:, :, 2]), table(qnorm, f_start)


def kernel(x, norm_mix, norm_ffn, pool_w, pool_scale, norm_kv, w_kv, b_f, w_q, w_o, ffn_gate, ffn_up,
           ffn_down, w_router, exp_gate, exp_up, exp_down, norm_final):
    batch, seq, d = x.shape
    n_tok = batch * seq
    assert d == D_MODEL and seq % TQ_ATTN == 0 and TQ_ATTN == 2 * TK_ATTN and TK_ATTN == TM_TOKENS
    assert seq // TK_ATTN <= HEAD_LANES
    row = lambda v: v.reshape(1, -1).astype(F32)
    x2d = x.reshape(n_tok, d)

    x2 = _layer0(x2d, seq, row(norm_mix[0]), row(norm_ffn[0]), pool_w[0].astype(BF16), row(pool_scale[0]),
                 ffn_gate[0].astype(BF16), ffn_up[0].astype(BF16), ffn_down[0].astype(BF16))

    wq = w_q[0].astype(BF16)
    wk = w_kv[:, :d].astype(BF16)
    wv = w_kv[:, d:2 * d].astype(BF16)
    wf = jnp.zeros((d, HEAD_LANES), F32).at[:, :N_PIECES * N_HEADS].set(
        jnp.tile(w_kv[:, 2 * d:], (1, N_PIECES))).astype(BF16)
    bf = jnp.zeros((1, HEAD_LANES), F32).at[0, :N_PIECES * N_HEADS].set(jnp.tile(b_f.astype(F32), N_PIECES))
    q, k, v, stats = _qkv(x2, batch, seq, row(norm_mix[1]), row(norm_kv), wq, wk, wv, wf, bf,
                          *_gate_constants())

    o = _attention(q, k, v, *_skip_tables(stats, batch, seq))

    wr = jnp.zeros((2 * N_EXPERTS, d), F32).at[:N_EXPERTS].set(w_router[0].T)
    wr_hi = wr.astype(BF16)
    wr_lo = (wr - wr_hi.astype(F32)).astype(BF16)
    x3, h3, meta, meta_t, counts = _oproj(o.reshape(n_tok, d), x2, w_o[0].astype(BF16), row(norm_ffn[1]), wr_hi, wr_lo)

    tm = TM_EXPERT
    n_tiles = n_tok * TOP_K // tm + N_EXPERTS
    counts = counts[:, 0].astype(jnp.int32)
    padded = (counts + tm - 1) // tm * tm
    group_end = jnp.cumsum(padded)
    group_start = group_end - padded
    expert = meta_t[0:2].astype(jnp.int32)
    slot = meta_t[2:4].astype(jnp.int32) + sum(
        jnp.where(expert == e, group_start[e], 0) for e in range(N_EXPERTS))
    n_used = (group_end[-1] // tm).astype(jnp.int32).reshape(1)
    tile_expert = jnp.sum(jnp.arange(n_tiles)[:, None] * tm >= group_end[None, :], axis=1)
    last_expert = jnp.max(jnp.where(counts > 0, jnp.arange(N_EXPERTS), 0))
    tile_expert = jnp.minimum(tile_expert, last_expert).astype(jnp.int32)
    order = jnp.argsort(slot.reshape(-1)).astype(jnp.int32)
    tile_ids = jnp.arange(n_tiles, dtype=jnp.int32)
    within = jnp.arange(tm, dtype=jnp.int32)[None, :]
    rank = (tile_ids * tm - group_start[tile_expert])[:, None] + within
    compact = (jnp.cumsum(counts) - counts)[tile_expert][:, None] + rank
    src = jnp.where(rank < counts[tile_expert][:, None],
                    order[jnp.minimum(compact, n_tok * TOP_K - 1)] % n_tok,
                    (tile_ids[:, None] * tm + within) % n_tok).reshape(-1)
    h_sorted = jnp.take(h3, src, axis=0, mode="clip")

    y = _moe(tile_expert, n_used, h_sorted, exp_gate[0], exp_up[0], exp_down[0])

    ya = jnp.take(y, slot[0], axis=0, mode="clip")
    yb = jnp.take(y, slot[1], axis=0, mode="clip")
    out = _final(x3, ya, yb, meta, row(norm_final))
    return out.reshape(batch, seq, d)
```

```python
import functools

import jax
import jax.numpy as jnp
import numpy as np
from jax import lax
from jax.experimental import pallas as pl
from jax.experimental.pallas import tpu as pltpu

F32 = jnp.float32
BF16 = jnp.bfloat16

D_MODEL = 1024
HEAD_DIM = 64
N_HEADS = D_MODEL // HEAD_DIM
HEAD_LANES = 128
POOL_WINDOWS = (2, 4, 8, 16)
POOL_GROUP = D_MODEL // len(POOL_WINDOWS)
POOL_HALO = 16
N_EXPERTS = 8
TOP_K = 2
EPS = 1e-5
NEG_INF = -1e30
LOG2E = 1.4426950408889634
STATS_ROWS = 8
SKIP_LOG2 = 152.0
NORM_SLACK = 1.02

N_PIECES = 3

TM_TOKENS = 512
TQ_ATTN = 1024
TK_ATTN = 512
TM_EXPERT = 1024
FF_CHUNK = 512
FF_SUB = 256

VMEM_LIMIT_BYTES = 56 * 1024 * 1024


def _params(*sem, flags=None):
    return pltpu.CompilerParams(dimension_semantics=sem, vmem_limit_bytes=VMEM_LIMIT_BYTES, flags=flags)


def _const_spec(shape):
    zeros = (0,) * len(shape)
    return pl.BlockSpec(shape, lambda *_: zeros, pipeline_mode=pl.Buffered(1))


def _rms(v):
    return v * lax.rsqrt(jnp.mean(v * v, axis=-1, keepdims=True) + EPS)


def _split3(v):
    hi = v.astype(BF16)
    r1 = v - hi.astype(F32)
    mid = r1.astype(BF16)
    lo = (r1 - mid.astype(F32)).astype(BF16)
    return hi, mid, lo


def _lower_tri(n, strict):
    r = lax.broadcasted_iota(jnp.int32, (n, n), 0)
    c = lax.broadcasted_iota(jnp.int32, (n, n), 1)
    return jnp.where((c < r) if strict else (c <= r), 1.0, 0.0).astype(BF16)


def _layer0_kernel(x_ref, xp_ref, gmix_ref, gffn_ref, pw_ref, ps_ref, wg_ref, wu_ref, wd_ref,
                   o_ref, *, tiles_per_seq):
    tm = x_ref.shape[0]
    t = pl.program_id(0) % tiles_per_seq
    x = x_ref[...]
    gmix = gmix_ref[...]
    h = _rms(x) * gmix
    hp = jnp.where(t == 0, 0.0, _rms(xp_ref[...]) * gmix)
    ext = jnp.concatenate([hp, h], axis=0)
    pos = t * tm + lax.broadcasted_iota(jnp.int32, (tm, 1), 0)
    mixed = []
    for g, w in enumerate(POOL_WINDOWS):
        lanes = slice(g * POOL_GROUP, (g + 1) * POOL_GROUP)
        win = ext[:, lanes]
        shift = 1
        while shift < w:
            win = win + pltpu.roll(win, shift, axis=0)
            shift *= 2
        cnt = jnp.minimum(pos + 1, w).astype(F32)
        pooled = win[POOL_HALO:, :] / cnt - h[:, lanes]
        mixed.append(jnp.dot(pooled.astype(BF16), pw_ref[g], preferred_element_type=F32))
    x1 = x + jnp.concatenate(mixed, axis=1) * ps_ref[...]
    h2 = (_rms(x1) * gffn_ref[...]).astype(BF16)
    gate = jnp.dot(h2, wg_ref[...], preferred_element_type=F32)
    up = jnp.dot(h2, wu_ref[...], preferred_element_type=F32)
    act = (gate * jax.nn.sigmoid(gate) * up).astype(BF16)
    o_ref[...] = x1 + jnp.dot(act, wd_ref[...], preferred_element_type=F32)


def _layer0(x2d, seq, gmix, gffn, pool_w, pool_scale, wg, wu, wd):
    n_tok, d = x2d.shape
    tm = TM_TOKENS
    tiles_per_seq = seq // tm
    halo_blocks = tm // POOL_HALO
    d_ff = wg.shape[1]
    return pl.pallas_call(
        functools.partial(_layer0_kernel, tiles_per_seq=tiles_per_seq),
        grid=(n_tok // tm,),
        in_specs=[
            pl.BlockSpec((tm, d), lambda i: (i, 0)),
            pl.BlockSpec((POOL_HALO, d), lambda i: (jnp.maximum(i * halo_blocks - 1, 0), 0)),
            _const_spec((1, d)), _const_spec((1, d)),
            _const_spec(pool_w.shape), _const_spec((1, d)),
            _const_spec((d, d_ff)), _const_spec((d, d_ff)), _const_spec((d_ff, d)),
        ],
        out_specs=pl.BlockSpec((tm, d), lambda i: (i, 0)),
        out_shape=jax.ShapeDtypeStruct((n_tok, d), F32),
        compiler_params=_params("arbitrary"),
        name="layer0",
    )(x2d, x2d, gmix, gffn, pool_w, pool_scale, wg, wu, wd)


def _qkv_kernel(x_ref, gq_ref, gkv_ref, wq_ref, wk_ref, wv_ref, wf_ref, bf_ref, pq_ref, pk_ref,
                cq_ref, ck_ref, cv_ref, hsum_ref, q_ref, k_ref, v_ref, stats_ref, carry_ref, *, tiles_per_seq):
    tm = x_ref.shape[0]

    @pl.when(pl.program_id(0) % tiles_per_seq == 0)
    def _():
        carry_ref[...] = jnp.zeros_like(carry_ref)

    xn = _rms(x_ref[...])
    hq = (xn * gq_ref[...]).astype(BF16)
    hkv = (xn * gkv_ref[...]).astype(BF16)

    z = jnp.dot(hkv, wf_ref[...], preferred_element_type=F32) + bf_ref[...]
    logf = jnp.minimum(z, 0.0) - jnp.log1p(jnp.exp(-jnp.abs(z)))
    tri = _lower_tri(tm, strict=False)
    fcum = carry_ref[...] + sum(jnp.dot(tri, p, preferred_element_type=F32) for p in _split3(logf))
    carry_ref[...] = fcum[tm - 1:tm, :]
    hi, mid, lo = _split3(fcum * LOG2E)
    lane = lax.broadcasted_iota(jnp.int32, fcum.shape, 1)
    pieces = jnp.where(lane < N_HEADS, hi, jnp.where(lane < 2 * N_HEADS, mid, lo))

    q = jnp.dot(hq, wq_ref[...], preferred_element_type=F32) * (HEAD_DIM ** -0.5 * LOG2E)
    k = jnp.dot(hkv, wk_ref[...], preferred_element_type=F32)
    v = jnp.dot(hkv, wv_ref[...], preferred_element_type=F32)
    gq = jnp.dot(pieces, pq_ref[...], preferred_element_type=F32) + cq_ref[...]
    gk = jnp.dot(pieces, pk_ref[...], preferred_element_type=F32) + ck_ref[...]
    low = lax.broadcasted_iota(jnp.int32, (tm, HEAD_LANES), 1) < HEAD_DIM
    for pair in range(N_HEADS // 2):
        data = slice(pair * HEAD_LANES, (pair + 1) * HEAD_LANES)
        for odd in range(2):
            h = 2 * pair + odd
            keep = jnp.logical_not(low) if odd else low
            q_ref[0, h] = jnp.where(keep, q[:, data], gq[:, data]).astype(BF16)
            k_ref[0, h] = jnp.where(keep, k[:, data], gk[:, data]).astype(BF16)
            v_ref[0, h] = jnp.where(keep, v[:, data], cv_ref[:, data]).astype(BF16)

    ksq = jnp.dot((k * k).astype(BF16), hsum_ref[...], preferred_element_type=F32)
    qsq = jnp.dot((q * q).astype(BF16), hsum_ref[...], preferred_element_type=F32)
    f2 = fcum * LOG2E
    rows = [jnp.max(ksq, axis=0, keepdims=True), jnp.max(qsq, axis=0, keepdims=True),
            f2[tm - 1:tm, :], f2[0:1, :]]
    stats_ref[0] = jnp.concatenate(rows + [jnp.zeros((STATS_ROWS - len(rows), HEAD_LANES), F32)], axis=0)


def _qkv(x2d, batch, seq, gq, gkv, wq, wk, wv, wf, bf, pq, pk, cq, ck, cv, hsum):
    n_tok, d = x2d.shape
    tm = TM_TOKENS
    tiles_per_seq = seq // tm
    wide = d
    head_spec = pl.BlockSpec((1, N_HEADS, tm, HEAD_LANES),
                             lambda i: (i // tiles_per_seq, 0, i % tiles_per_seq, 0))
    head_shape = jax.ShapeDtypeStruct((batch, N_HEADS, seq, HEAD_LANES), BF16)
    return pl.pallas_call(
        functools.partial(_qkv_kernel, tiles_per_seq=tiles_per_seq),
        grid=(n_tok // tm,),
        in_specs=[
            pl.BlockSpec((tm, d), lambda i: (i, 0)),
            _const_spec((1, d)), _const_spec((1, d)),
            _const_spec((d, d)), _const_spec((d, d)), _const_spec((d, d)),
            _const_spec((d, HEAD_LANES)), _const_spec((1, HEAD_LANES)),
            _const_spec((HEAD_LANES, wide)), _const_spec((HEAD_LANES, wide)),
            _const_spec((1, wide)), _const_spec((1, wide)), _const_spec((1, wide)),
            _const_spec((d, HEAD_LANES)),
        ],
        out_specs=[head_spec, head_spec, head_spec,
                   pl.BlockSpec((1, STATS_ROWS, HEAD_LANES), lambda i: (i, 0, 0))],
        out_shape=[head_shape, head_shape, head_shape,
                   jax.ShapeDtypeStruct((n_tok // tm, STATS_ROWS, HEAD_LANES), F32)],
        scratch_shapes=[pltpu.VMEM((1, HEAD_LANES), F32)],
        compiler_params=_params("arbitrary"),
        name="qkv",
    )(x2d, gq, gkv, wq, wk, wv, wf, bf, pq, pk, cq, ck, cv, hsum)


def _attn_kernel(q_ref, k_ref, v_ref, kstat_ref, qstat_ref, o_ref, p_buf, corr_buf, m_buf, acc_buf, *, tk):
    tq = q_ref.shape[2]
    qi = pl.program_id(2)
    chunks_per_q = tq // tk
    assert chunks_per_q == 2
    n_below = qi * chunks_per_q
    heads = range(2)
    tri = (lax.broadcasted_iota(jnp.int32, (tk, tk), 0) >= lax.broadcasted_iota(jnp.int32, (tk, tk), 1))
    lane = lax.broadcasted_iota(jnp.int32, (1, HEAD_LANES), 1)

    def scores(hh, j, rows=slice(None)):
        k = k_ref[0, hh, pl.ds(pl.multiple_of(j * tk, tk), tk), :]
        return lax.dot_general(q_ref[0, hh, rows, :], k, (((1,), (1,)), ((), ())),
                               preferred_element_type=F32)

    def mask_top(s):
        masked = jnp.where(tri, s[:tk], NEG_INF)
        return masked if s.shape[0] == tk else jnp.concatenate([masked, s[tk:]], axis=0)

    def softmax_chunk(s, m):
        blocks = [s[:, c:c + HEAD_LANES] for c in range(0, s.shape[1], HEAD_LANES)]
        m_new = jnp.maximum(m, jnp.max(functools.reduce(jnp.maximum, blocks), axis=-1, keepdims=True))
        p = jnp.concatenate([jnp.exp2(b - m_new) for b in blocks], axis=1)
        return m_new, p.astype(BF16), jnp.exp2(m - m_new)

    def value_matmul(hh, j, p):
        v = v_ref[0, hh, pl.ds(pl.multiple_of(j * tk, tk), tk), :]
        return jnp.dot(p, v, preferred_element_type=F32)

    def prepare(hh, j, slot, valid=None):
        m_old = m_buf[hh]
        m, p, corr = softmax_chunk(scores(hh, j), m_old)
        m_buf[hh] = m if valid is None else jnp.where(valid, m, m_old)
        p_buf[hh, slot] = p
        corr_buf[hh, slot] = corr

    def consume(hh, j, slot):
        acc_buf[hh] = acc_buf[hh] * corr_buf[hh, slot] + value_matmul(hh, j, p_buf[hh, slot])

    first_needed = []
    for hh in heads:
        m, p, _ = softmax_chunk(mask_top(scores(hh, n_below)), jnp.full((tq, HEAD_LANES), NEG_INF, F32))
        acc = value_matmul(hh, n_below, p)
        m_low, p, corr = softmax_chunk(mask_top(scores(hh, n_below + 1, rows=slice(tk, tq))), m[tk:])
        m = jnp.concatenate([m[:tk], m_low], axis=0)
        m_buf[hh] = m
        acc_buf[hh] = jnp.concatenate([acc[:tk], acc[tk:] * corr + value_matmul(hh, n_below + 1, p)], axis=0)

        kst = kstat_ref[0, hh]
        qst = qstat_ref[0, hh]
        mine = lane == qi
        qn = jnp.sum(jnp.where(mine, qst[0:1], 0.0), axis=1, keepdims=True)
        f0 = jnp.sum(jnp.where(mine, qst[1:2], 0.0), axis=1, keepdims=True)
        bound = qn * kst[0:1] * NORM_SLACK + (f0 - kst[1:2])
        floor = jnp.min(m, axis=0, keepdims=True) - SKIP_LOG2
        dead = (bound < floor) & (lane < n_below)
        first_needed.append(jnp.sum(dead.astype(jnp.int32)))

    starts = [f // 2 * 2 for f in first_needed]
    joint_start = jnp.maximum(starts[0], starts[1])

    for hh in heads:
        prepare(hh, jnp.maximum(n_below - 1, 0), 0, valid=qi > 0)

    def pair_body(which, top):
        def body(jj, _):
            newest = top - 1 - 2 * jj
            for slot in range(2):
                for hh in which:
                    prepare(hh, jnp.maximum(newest - slot - 1, 0), 1 - slot)
                    consume(hh, newest - slot, slot)
            return 0
        return body

    lax.fori_loop(0, (n_below - joint_start) // 2, pair_body(heads, n_below), 0)
    for hh in heads:
        lax.fori_loop(0, (joint_start - starts[hh]) // 2, pair_body((hh,), joint_start), 0)
    outs = [acc_buf[hh] for hh in heads]
    even = outs[0] / outs[0][:, HEAD_DIM:HEAD_DIM + 1]
    odd = outs[1] / outs[1][:, 0:1]
    lane = lax.broadcasted_iota(jnp.int32, even.shape, 1)
    o_ref[0] = jnp.where(lane < HEAD_DIM, even, odd).astype(o_ref.dtype)


def _attention(q, k, v, kstat, qstat):
    batch, _, seq, _ = q.shape
    tq, tk = TQ_ATTN, TK_ATTN
    pair_kv = pl.BlockSpec((1, 2, seq, HEAD_LANES), lambda b, hp, i: (b, hp, 0, 0))
    pair_stat = pl.BlockSpec((1, 2, STATS_ROWS, HEAD_LANES), lambda b, hp, i: (b, hp, 0, 0))
    return pl.pallas_call(
        functools.partial(_attn_kernel, tk=tk),
        grid=(batch, N_HEADS // 2, seq // tq),
        in_specs=[pl.BlockSpec((1, 2, tq, HEAD_LANES), lambda b, hp, i: (b, hp, i, 0)), pair_kv, pair_kv,
                  pair_stat, pair_stat],
        out_specs=pl.BlockSpec((1, tq, HEAD_LANES), lambda b, hp, i: (b, i, hp)),
        out_shape=jax.ShapeDtypeStruct((batch, seq, D_MODEL), BF16),
        scratch_shapes=[pltpu.VMEM((2, 2, tq, tk), BF16),
                        pltpu.VMEM((2, 2, tq, HEAD_LANES), F32),
                        pltpu.VMEM((2, tq, HEAD_LANES), F32),
                        pltpu.VMEM((2, tq, HEAD_LANES), F32)],
        compiler_params=_params("arbitrary", "arbitrary", "arbitrary"),
        name="attn",
    )(q, k, v, kstat, qstat)


META_LANES = 8


def _oproj_kernel(o_ref, x_ref, wo_ref, g_ref, wrh_ref, wrl_ref,
                  x3_ref, h3_ref, meta_ref, cnt_ref, carry_ref):
    tm = x_ref.shape[0]

    @pl.when(pl.program_id(0) == 0)
    def _():
        carry_ref[...] = jnp.zeros_like(carry_ref)

    x3 = x_ref[...] + jnp.dot(o_ref[...], wo_ref[...], preferred_element_type=F32)
    x3_ref[...] = x3
    h3 = _rms(x3) * g_ref[...]
    h_hi = h3.astype(BF16)
    h_lo = (h3 - h_hi.astype(F32)).astype(BF16)
    h3_ref[...] = h_hi
    nt = (((1,), (1,)), ((), ()))
    logits = (lax.dot_general(wrh_ref[...], h_hi, nt, preferred_element_type=F32)
              + lax.dot_general(wrh_ref[...], h_lo, nt, preferred_element_type=F32)
              + lax.dot_general(wrl_ref[...], h_hi, nt, preferred_element_type=F32))
    lg = logits[:N_EXPERTS]
    expert = lax.broadcasted_iota(jnp.int32, lg.shape, 0)
    m1 = jnp.max(lg, axis=0, keepdims=True)
    i1 = jnp.min(jnp.where(lg == m1, expert, N_EXPERTS), axis=0, keepdims=True)
    lg2 = jnp.where(expert == i1, -jnp.inf, lg)
    m2 = jnp.max(lg2, axis=0, keepdims=True)
    i2 = jnp.min(jnp.where(lg2 == m2, expert, N_EXPERTS), axis=0, keepdims=True)
    e2 = jnp.exp(m2 - m1)
    w1 = 1.0 / (1.0 + e2)
    w2 = e2 / (1.0 + e2)
    sel1 = expert == i1
    sel2 = expert == i2
    chosen = jnp.where(sel1 | sel2, 1.0, 0.0)
    chosen16 = jnp.concatenate([chosen, jnp.zeros_like(chosen)], axis=0).astype(BF16)
    row_t = lax.broadcasted_iota(jnp.int32, (tm, tm), 0)
    col_t = lax.broadcasted_iota(jnp.int32, (tm, tm), 1)
    earlier = jnp.where(row_t < col_t, 1.0, 0.0).astype(BF16)
    before = carry_ref[...][:, 0:1] + jnp.dot(chosen16, earlier, preferred_element_type=F32)[:N_EXPERTS]
    carry_ref[...] = carry_ref[...] + jnp.sum(chosen, axis=1, keepdims=True)
    cnt_ref[...] = carry_ref[...]
    r1 = jnp.sum(jnp.where(sel1, before, 0.0), axis=0, keepdims=True)
    r2 = jnp.sum(jnp.where(sel2, before, 0.0), axis=0, keepdims=True)
    meta_t = jnp.concatenate([i1.astype(F32), i2.astype(F32), r1, r2, w1, w2,
                              jnp.zeros((HEAD_LANES - 6, tm), F32)], axis=0)
    meta_ref[...] = meta_t.T[:, :META_LANES]


def _oproj(o2d, x2d, wo, g, wr_hi, wr_lo):
    n_tok, d = x2d.shape
    tm = TM_TOKENS
    row = lambda i: (i, 0)
    return pl.pallas_call(
        _oproj_kernel,
        grid=(n_tok // tm,),
        in_specs=[
            pl.BlockSpec((tm, d), row), pl.BlockSpec((tm, d), row),
            _const_spec((d, d)), _const_spec((1, d)),
            _const_spec(wr_hi.shape), _const_spec(wr_lo.shape),
        ],
        out_specs=[pl.BlockSpec((tm, d), row), pl.BlockSpec((tm, d), row),
                   pl.BlockSpec((tm, META_LANES), row),
                   pl.BlockSpec((N_EXPERTS, HEAD_LANES), lambda i: (0, 0))],
        out_shape=[jax.ShapeDtypeStruct((n_tok, d), F32), jax.ShapeDtypeStruct((n_tok, d), BF16),
                   jax.ShapeDtypeStruct((n_tok, META_LANES), F32),
                   jax.ShapeDtypeStruct((N_EXPERTS, HEAD_LANES), F32)],
        scratch_shapes=[pltpu.VMEM((N_EXPERTS, HEAD_LANES), F32)],
        compiler_params=_params("arbitrary"),
        name="oproj",
    )(o2d, x2d, wo, g, wr_hi, wr_lo)


def _moe_kernel(te_ref, nu_ref, h_ref, wg_ref, wu_ref, wd_ref, y_ref):
    @pl.when(pl.program_id(1) == 0)
    def _():
        y_ref[...] = jnp.zeros_like(y_ref)

    @pl.when(pl.program_id(0) < nu_ref[0])
    def _():
        h = h_ref[...]
        for s0 in range(0, wg_ref.shape[2], FF_SUB):
            sub = slice(s0, s0 + FF_SUB)
            gate = jnp.dot(h, wg_ref[0, :, sub].astype(BF16), preferred_element_type=F32)
            up = jnp.dot(h, wu_ref[0, :, sub].astype(BF16), preferred_element_type=F32)
            act = (gate * jax.nn.sigmoid(gate) * up).astype(BF16)
            y_ref[...] += jnp.dot(act, wd_ref[0, sub, :].astype(BF16), preferred_element_type=F32)


def _moe(tile_expert, n_used, h_sorted, wg, wu, wd):
    n_slots, d = h_sorted.shape
    tm = TM_EXPERT
    fc = FF_CHUNK
    n_chunks = wg.shape[2] // fc
    def row_map(i, c, te, nu):
        return (jnp.minimum(i, nu[0] - 1), 0)
    def chunk(i, c, te, nu):
        return jnp.where(i < nu[0], c, n_chunks - 1)
    return pl.pallas_call(
        _moe_kernel,
        grid_spec=pltpu.PrefetchScalarGridSpec(
            num_scalar_prefetch=2,
            grid=(n_slots // tm, n_chunks),
            in_specs=[
                pl.BlockSpec((tm, d), row_map),
                pl.BlockSpec((1, d, fc), lambda i, c, te, nu: (te[i], 0, chunk(i, c, te, nu))),
                pl.BlockSpec((1, d, fc), lambda i, c, te, nu: (te[i], 0, chunk(i, c, te, nu))),
                pl.BlockSpec((1, fc, d), lambda i, c, te, nu: (te[i], chunk(i, c, te, nu), 0)),
            ],
            out_specs=pl.BlockSpec((tm, d), lambda i, c, te, nu: (i, 0)),
        ),
        out_shape=jax.ShapeDtypeStruct((n_slots, d), F32),
        compiler_params=_params("arbitrary", "arbitrary"),
        name="moe",
    )(tile_expert, n_used, h_sorted, wg, wu, wd)


def _final_kernel(x_ref, ya_ref, yb_ref, meta_ref, g_ref, o_ref):
    meta = meta_ref[...]
    x = x_ref[...] + meta[:, 4:5] * ya_ref[0] + meta[:, 5:6] * yb_ref[0]
    o_ref[...] = _rms(x) * g_ref[...]


def _final(x2d, yab, meta, g):
    n_tok, d = x2d.shape
    tm = TM_TOKENS
    row = lambda i: (i, 0)
    big = pl.BlockSpec((tm, d), row)
    ya, yb = yab, yab
    return pl.pallas_call(
        _final_kernel,
        grid=(n_tok // tm,),
        in_specs=[big, pl.BlockSpec((1, tm, d), lambda i: (0, i, 0)), pl.BlockSpec((1, tm, d), lambda i: (1, i, 0)),
                  pl.BlockSpec((tm, META_LANES), row), _const_spec((1, d))],
        out_specs=big,
        out_shape=jax.ShapeDtypeStruct((n_tok, d), F32),
        compiler_params=_params("arbitrary"),
        name="final",
    )(x2d, ya, yb, meta, g)


def _gate_constants():
    wide = D_MODEL
    pq = np.zeros((HEAD_LANES, wide), np.float32)
    pk = np.zeros((HEAD_LANES, wide), np.float32)
    cq = np.zeros((1, wide), np.float32)
    ck = np.zeros((1, wide), np.float32)
    cv = np.zeros((1, wide), np.float32)
    for h in range(N_HEADS):
        base = (h // 2) * HEAD_LANES + (HEAD_DIM if h % 2 == 0 else 0)
        cv[0, base] = 1.0
        for piece in range(N_PIECES):
            r = piece * N_HEADS + h
            pq[r, base + piece] = 1.0
            ck[0, base + piece] = 1.0
            cq[0, base + N_PIECES + piece] = 1.0
            pk[r, base + N_PIECES + piece] = -1.0
    hsum = np.zeros((D_MODEL, HEAD_LANES), np.float32)
    hsum[np.arange(D_MODEL), np.arange(D_MODEL) // HEAD_DIM] = 1.0
    return (jnp.asarray(pq, BF16), jnp.asarray(pk, BF16), jnp.asarray(cq), jnp.asarray(ck), jnp.asarray(cv),
            jnp.asarray(hsum, BF16))


def _skip_tables(stats, batch, seq):
    tiles = seq // TM_TOKENS
    per_q = TQ_ATTN // TM_TOKENS
    st = stats.reshape(batch, tiles, STATS_ROWS, HEAD_LANES)[..., :N_HEADS]
    st = jnp.swapaxes(st, 1, 3)
    knorm = lax.cummax(jnp.sqrt(st[:, :, 0]), axis=2)
    qnorm = jnp.sqrt(jnp.max(st[:, :, 1].reshape(batch, N_HEADS, tiles // per_q, per_q), axis=-1))
    f_start = st[:, :, 3].reshape(batch, N_HEADS, tiles // per_q, per_q)[..., 0]

    def table(row0, row1):
        t = jnp.stack([row0, row1], axis=2)
        return jnp.pad(t, ((0, 0), (0, 0), (0, STATS_ROWS - 2), (0, HEAD_LANES - t.shape[-1])))

    return table(knorm, st[:, :, 2]), table(qnorm, f_start)


def kernel(x, norm_mix, norm_ffn, pool_w, pool_scale, norm_kv, w_kv, b_f, w_q, w_o, ffn_gate, ffn_up,
           ffn_down, w_router, exp_gate, exp_up, exp_down, norm_final):
    batch, seq, d = x.shape
    n_tok = batch * seq
    assert d == D_MODEL and seq % TQ_ATTN == 0 and TQ_ATTN == 2 * TK_ATTN and TK_ATTN == TM_TOKENS
    assert seq // TK_ATTN <= HEAD_LANES
    row = lambda v: v.reshape(1, -1).astype(F32)
    x2d = x.reshape(n_tok, d)

    x2 = _layer0(x2d, seq, row(norm_mix[0]), row(norm_ffn[0]), pool_w[0].astype(BF16), row(pool_scale[0]),
                 ffn_gate[0].astype(BF16), ffn_up[0].astype(BF16), ffn_down[0].astype(BF16))

    wq = w_q[0].astype(BF16)
    wk = w_kv[:, :d].astype(BF16)
    wv = w_kv[:, d:2 * d].astype(BF16)
    wf = jnp.zeros((d, HEAD_LANES), F32).at[:, :N_PIECES * N_HEADS].set(
        jnp.tile(w_kv[:, 2 * d:], (1, N_PIECES))).astype(BF16)
    bf = jnp.zeros((1, HEAD_LANES), F32).at[0, :N_PIECES * N_HEADS].set(jnp.tile(b_f.astype(F32), N_PIECES))
    q, k, v, stats = _qkv(x2, batch, seq, row(norm_mix[1]), row(norm_kv), wq, wk, wv, wf, bf,
                          *_gate_constants())

    o = _attention(q, k, v, *_skip_tables(stats, batch, seq))

    wr = jnp.zeros((2 * N_EXPERTS, d), F32).at[:N_EXPERTS].set(w_router[0].T)
    wr_hi = wr.astype(BF16)
    wr_lo = (wr - wr_hi.astype(F32)).astype(BF16)
    x3, h3, meta, counts = _oproj(o.reshape(n_tok, d), x2, w_o[0].astype(BF16), row(norm_ffn[1]), wr_hi, wr_lo)

    tm = TM_EXPERT
    n_tiles = n_tok * TOP_K // tm + N_EXPERTS
    counts = counts[:, 0].astype(jnp.int32)
    padded = (counts + tm - 1) // tm * tm
    group_end = jnp.cumsum(padded)
    group_start = group_end - padded
    expert = meta[:, 0:2].astype(jnp.int32)
    slot = group_start[expert] + meta[:, 2:4].astype(jnp.int32)
    n_used = (group_end[-1] // tm).astype(jnp.int32).reshape(1)
    tile_expert = jnp.sum(jnp.arange(n_tiles)[:, None] * tm >= group_end[None, :], axis=1)
    last_expert = jnp.max(jnp.where(counts > 0, jnp.arange(N_EXPERTS), 0))
    tile_expert = jnp.minimum(tile_expert, last_expert).astype(jnp.int32)
    order = jnp.argsort(slot.reshape(-1)).astype(jnp.int32)
    tile_ids = jnp.arange(n_tiles, dtype=jnp.int32)
    within = jnp.arange(tm, dtype=jnp.int32)[None, :]
    rank = (tile_ids * tm - group_start[tile_expert])[:, None] + within
    compact = (jnp.cumsum(counts) - counts)[tile_expert][:, None] + rank
    src = jnp.where(rank < counts[tile_expert][:, None],
                    order[jnp.minimum(compact, n_tok * TOP_K - 1)] // TOP_K,
                    (tile_ids[:, None] * tm + within) % n_tok).reshape(-1)
    h_sorted = jnp.take(h3, src, axis=0, mode="clip")

    y = _moe(tile_expert, n_used, h_sorted, exp_gate[0], exp_up[0], exp_down[0])

    yab = jnp.take(y, slot.T.reshape(-1), axis=0, mode="clip").reshape(TOP_K, n_tok, d)
    out = _final(x3, yab, meta, row(norm_final))
    return out.reshape(batch, seq, d)
```
